```python
import math
import jax
import jax.numpy as jnp
from jax import lax
import numpy as np

D_MODEL = 1024
BATCH = 32
SEQ = 2048
DEPTH = 2

CTX_LEN = 256
GRID_W = 64
NORM_EPS = 1e-6

RW_HEADS = 8
RW_HEAD = 64
RW_DIM = 512
RW_DECAY_LORA = 64
RW_ICL_LORA = 64
RW_GATE_LORA = 128
RW_GN_EPS = 64e-5
RW_SPLITS = (512, 512, 512, 64, 64, 64, 64, 128)
RW_COLS = 1920

SSM_HEADS = 16
SSM_HEAD = 64
SSM_DIM = 1024
SSM_GROUPS = 2
SSM_STATE = 128
SSM_CONV = 5
SSM_CHUNK = 64
XBC_DIM = 1536
EVEN_COLS = 4512
MIX_DIM = 1536

MLA_HEADS = 16
MLA_Q_LORA = 384
MLA_KV_LORA = 256
MLA_NOPE = 64
MLA_ROPE = 32
MLA_V = 64
ODD_COLS = 672
ROPE_THETA = 10000.0
Q_BLOCK = 128

N_EXPERTS = 64
TOP_K = 6
N_EXPERT_GROUPS = 8
TOPK_GROUPS = 4
EXPERT_FF = 256
SHARED_FF = 256
ROUTED_SCALE = 2.5

kernel_name = "hybrid_rwkv7_mamba2_mla_moe_prefix_dit"


def split_sizes(u, sizes):
    return jnp.split(u, np.cumsum(sizes)[:-1].tolist(), axis=-1)


def rms_normalize(u, eps=NORM_EPS):
    u32 = u.astype(jnp.float32)
    return (u32 * lax.rsqrt(jnp.mean(u32 * u32, axis=-1, keepdims=True) + eps)).astype(u.dtype)


def rmsnorm(u, g):
    return rms_normalize(u) * g


def modulate(h, shift, scale):
    return h * (1 + scale) + shift


def axial_rope_tables(n_tokens, dtype):
    rows = n_tokens // GRID_W
    r_idx, c_idx = jnp.meshgrid(jnp.arange(rows), jnp.arange(GRID_W), indexing='ij')
    r_idx = r_idx.reshape(-1).astype(jnp.float32)
    c_idx = c_idx.reshape(-1).astype(jnp.float32)
    axis_dim = MLA_ROPE // 2
    inv_freq = ROPE_THETA ** (-jnp.arange(0, axis_dim, 2, dtype=jnp.float32) / axis_dim)
    ang = jnp.concatenate([r_idx[:, None] * inv_freq, c_idx[:, None] * inv_freq], axis=-1)
    return jnp.cos(ang).astype(dtype), jnp.sin(ang).astype(dtype)


def apply_rope(u, cos, sin):
    u1, u2 = jnp.split(u, 2, axis=-1)
    return jnp.concatenate([u1 * cos - u2 * sin, u1 * sin + u2 * cos], axis=-1)


def bidir_shift_mix(p, mu):
    zero = jnp.zeros_like(p[:, :1])
    prev = jnp.concatenate([zero, p[:, :-1]], axis=1)
    nxt = jnp.concatenate([p[:, 1:], zero], axis=1)
    return p + mu * (0.5 * (prev + nxt) - p)


def depthwise_conv_centred(u, w, b):
    k = w.shape[0]
    out = lax.conv_general_dilated(u, w[:, None, :], window_strides=(1,), padding=[(k // 2, k // 2)],
                                   dimension_numbers=('NWC', 'WIO', 'NWC'), feature_group_count=u.shape[-1])
    return out + b


def rwkv7_scan(r, w, k, v, kk, b, s0, reverse):
    def step(s, inp):
        r_t, w_t, k_t, v_t, kk_t, b_t = inp
        sa = jnp.einsum('bhvk,bhk->bhv', s, kk_t)
        s = s * w_t[:, :, None, :] - sa[..., None] * b_t[:, :, None, :] + v_t[..., None] * k_t[:, :, None, :]
        return s, jnp.einsum('bhvk,bhk->bhv', s, r_t)
    xs = tuple(jnp.moveaxis(u, 1, 0) for u in (r, w, k, v, kk, b))
    s_fin, ys = lax.scan(step, s0, xs, reverse=reverse)
    return jnp.moveaxis(ys, 0, 1), s_fin


def rwkv7_branch(cols, s_fwd, s_bwd, mu, w0, w2, a0, a2, g2, k_k, k_a, r_k, gn_w, gn_b):
    bsz, t = cols.shape[:2]

    def hd(u):
        return u.reshape(*u.shape[:-1], RW_HEADS, RW_HEAD)

    cols = bidir_shift_mix(cols, mu)
    r, k, v, xw_f, xw_b, xa_f, xa_b, xg = split_sizes(cols, RW_SPLITS)
    kk = hd(k * k_k).astype(jnp.float32)
    kk = (kk * lax.rsqrt(jnp.sum(kk * kk, axis=-1, keepdims=True) + 1e-12)).astype(k.dtype)
    ys, ks, finals = [], [], []
    for d, (xw, xa, s0, rev) in enumerate(((xw_f, xa_f, s_fwd, False), (xw_b, xa_b, s_bwd, True))):
        w_log = -jax.nn.softplus(-(w0[d] + jnp.tanh(xw) @ w2[d])) - 0.5
        decay = jnp.exp(-jnp.exp(w_log))
        a = jax.nn.sigmoid(a0[d] + xa @ a2[d])
        k_d = k * (1 + (a - 1) * k_a)
        y, s_fin = rwkv7_scan(hd(r), hd(decay), hd(k_d), hd(v), kk, kk * hd(a), s0, rev)
        ys.append(y)
        ks.append(k_d)
        finals.append(s_fin)
    y = ys[0] + ys[1]
    mean = jnp.mean(y, axis=-1, keepdims=True)
    var = jnp.mean(jnp.square(y - mean), axis=-1, keepdims=True)
    y = ((y - mean) * lax.rsqrt(var + RW_GN_EPS)).astype(r.dtype) * hd(gn_w) + hd(gn_b)
    bonus = jnp.sum(hd(r) * hd(ks[0] + ks[1]) * hd(r_k), axis=-1, keepdims=True) * hd(v)
    g = jax.nn.sigmoid(xg) @ g2
    out = (y + bonus).reshape(bsz, t, RW_DIM) * g
    return out, finals[0], finals[1]


def ssd_scan(xs, dt, a_neg, bm, cm, s0):
    b, t, h, p = xs.shape
    g, n = bm.shape[2], bm.shape[3]
    j = h // g
    nc = t // SSM_CHUNK
    cl = SSM_CHUNK
    da = (dt.astype(jnp.float32) * a_neg.astype(jnp.float32)).reshape(b, nc, cl, h)
    cs = jnp.cumsum(da, axis=2)
    xdt = (xs * dt[..., None]).reshape(b, nc, cl, g, j, p)
    bc = bm.reshape(b, nc, cl, g, n)
    cc = cm.reshape(b, nc, cl, g, n)
    causal = jnp.tril(jnp.ones((cl, cl), bool))
    seg = cs[:, :, :, None, :] - cs[:, :, None, :, :]
    lmat = jnp.exp(jnp.where(causal[:, :, None], seg, -jnp.inf)).reshape(b, nc, cl, cl, g, j).astype(xs.dtype)
    cb = jnp.einsum('bclgn,bcsgn->bclsg', cc, bc)
    y_diag = jnp.einsum('bclsg,bclsgj,bcsgjp->bclgjp', cb, lmat, xdt)
    decay_to_end = jnp.exp(cs[:, :, -1:, :] - cs).reshape(b, nc, cl, g, j).astype(xs.dtype)
    states = jnp.einsum('bclgn,bclgj,bclgjp->bcgjpn', bc, decay_to_end, xdt).reshape(b, nc, h, p, n)
    chunk_decay = jnp.exp(cs[:, :, -1, :])

    def step(s, inp):
        dec, st = inp
        return s * dec[:, :, None, None] + st, s

    s_fin, s_in = lax.scan(step, s0, (jnp.moveaxis(chunk_decay, 1, 0),
                                      jnp.moveaxis(states.astype(jnp.float32), 1, 0)))
    s_in = jnp.moveaxis(s_in, 0, 1).reshape(b, nc, g, j, p, n).astype(xs.dtype)
    decay_from_start = jnp.exp(cs).reshape(b, nc, cl, g, j).astype(xs.dtype)
    y_off = jnp.einsum('bclgn,bcgjpn,bclgj->bclgjp', cc, s_in, decay_from_start)
    return (y_diag + y_off).reshape(b, t, h, p), s_fin


def mamba2_branch(z, xbc, dt_raw, s_fwd, s_bwd, conv_w, conv_b, dt_bias, a_log, d_skip, norm_w):
    bsz, t = z.shape[:2]
    xbc = jax.nn.silu(depthwise_conv_centred(xbc, conv_w, conv_b))
    xs, bm, cm = split_sizes(xbc, (SSM_DIM, SSM_GROUPS * SSM_STATE, SSM_GROUPS * SSM_STATE))
    xs = xs.reshape(bsz, t, SSM_HEADS, SSM_HEAD)
    bm = bm.reshape(bsz, t, SSM_GROUPS, SSM_STATE)
    cm = cm.reshape(bsz, t, SSM_GROUPS, SSM_STATE)
    dt = jax.nn.softplus(dt_raw.reshape(bsz, t, 2, SSM_HEADS) + dt_bias)
    a_neg = -jnp.exp(a_log)
    y_f, s_f = ssd_scan(xs, dt[:, :, 0], a_neg[0], bm, cm, s_fwd)
    flip = lambda u: jnp.flip(u, axis=1)
    y_b, s_b = ssd_scan(flip(xs), flip(dt[:, :, 1]), a_neg[1], flip(bm), flip(cm), s_bwd)
    y = y_f + flip(y_b) + xs * d_skip[:, None]
    y = (y.reshape(bsz, t, SSM_DIM) * jax.nn.silu(z)).reshape(bsz, t, SSM_GROUPS, SSM_DIM // SSM_GROUPS)
    y = rms_normalize(y).reshape(bsz, t, SSM_DIM) * norm_w
    return y, s_f, s_b


def even_mixer(h_lat, h_ctx, w_in, w_out, rw_mu, rw_w0, rw_w2, rw_a0, rw_a2, rw_g2, rw_kk, rw_ka, rw_rk,
               rw_gn_w, rw_gn_b, conv_w, conv_b, dt_bias, a_log, d_skip, ssm_norm_w):
    bsz = h_lat.shape[0]
    rw_f = rw_b = jnp.zeros((bsz, RW_HEADS, RW_HEAD, RW_HEAD), jnp.float32)
    ss_f = ss_b = jnp.zeros((bsz, SSM_HEADS, SSM_HEAD, SSM_STATE), jnp.float32)
    outs = []
    for h in (h_ctx, h_lat):
        rw_cols, z, xbc, dt_raw = split_sizes(h @ w_in, (RW_COLS, SSM_DIM, XBC_DIM, 2 * SSM_HEADS))
        y_rw, rw_f, rw_b = rwkv7_branch(rw_cols, rw_f, rw_b, rw_mu, rw_w0, rw_w2, rw_a0, rw_a2, rw_g2,
                                        rw_kk, rw_ka, rw_rk, rw_gn_w, rw_gn_b)
        y_ss, ss_f, ss_b = mamba2_branch(z, xbc, dt_raw, ss_f, ss_b, conv_w, conv_b, dt_bias, a_log,
                                         d_skip, ssm_norm_w)
        outs.append(jnp.concatenate([y_rw, y_ss], axis=-1) @ w_out)
    return outs[1], outs[0]


def block_attention(qn, qr, kn, kr, v, scale):
    b, t, h, _ = qn.shape
    nb = t // Q_BLOCK

    def to_blocks(u):
        return jnp.moveaxis(u.reshape(b, nb, Q_BLOCK, *u.shape[2:]), 1, 0)

    def one_block(q_pair):
        qn_i, qr_i = q_pair
        s = jnp.einsum('bqhd,bkhd->bhqk', qn_i, kn) + jnp.einsum('bqhr,bkr->bhqk', qr_i, kr)
        pr = jax.nn.softmax(s.astype(jnp.float32) * scale, axis=-1).astype(v.dtype)
        return jnp.einsum('bhqk,bkhd->bqhd', pr, v)

    out = lax.map(one_block, (to_blocks(qn), to_blocks(qr)))
    return jnp.moveaxis(out, 0, 1).reshape(b, t, h, v.shape[-1])


def mla_mixer(h_lat, h_ctx, w_in, q_norm, q_up, kv_norm, kv_up, w_out, cos, sin, need_ctx):
    scale = (MLA_NOPE + MLA_ROPE) ** -0.5

    def split_heads(u, width):
        return u.reshape(u.shape[0], u.shape[1], MLA_HEADS, width)

    def queries(cq):
        q = split_heads(rmsnorm(cq, q_norm) @ q_up, MLA_NOPE + MLA_ROPE)
        return q[..., :MLA_NOPE], q[..., MLA_NOPE:]

    def keys_values(ckv):
        kv = split_heads(rmsnorm(ckv, kv_norm) @ kv_up, MLA_NOPE + MLA_V)
        return kv[..., :MLA_NOPE], kv[..., MLA_NOPE:]

    bsz, t = h_lat.shape[:2]
    cq_l, ckv_l, kr_l = split_sizes(h_lat @ w_in, (MLA_Q_LORA, MLA_KV_LORA, MLA_ROPE))
    cq_c, ckv_c, kr_c = split_sizes(h_ctx @ w_in, (MLA_Q_LORA, MLA_KV_LORA, MLA_ROPE))
    qn_l, qr_l = queries(cq_l)
    qr_l = apply_rope(qr_l, cos[:, None, :], sin[:, None, :])
    kr_l = apply_rope(kr_l, cos, sin)
    kn_l, v_l = keys_values(ckv_l)
    kn_c, v_c = keys_values(ckv_c)
    kn = jnp.concatenate([kn_c, kn_l], axis=1)
    kr = jnp.concatenate([kr_c, kr_l], axis=1)
    v = jnp.concatenate([v_c, v_l], axis=1)
    out_l = block_attention(qn_l, qr_l, kn, kr, v, scale).reshape(bsz, t, MLA_HEADS * MLA_V) @ w_out
    out_c = None
    if need_ctx:
        qn_c, qr_c = queries(cq_c)
        out_c = block_attention(qn_c, qr_c, kn_c, kr_c, v_c, scale).reshape(
            bsz, h_ctx.shape[1], MLA_HEADS * MLA_V) @ w_out
    return out_l, out_c


def moe_ffn(tok, router_w, router_bias, w1, w3, w2, sw1, sw3, sw2):
    logits = (tok @ router_w).astype(jnp.float32)
    scores = jax.nn.sigmoid(logits)
    sel = scores + router_bias
    grp = sel.reshape(-1, N_EXPERT_GROUPS, N_EXPERTS // N_EXPERT_GROUPS)
    grp_score = jnp.sum(lax.top_k(grp, 2)[0], axis=-1)
    _, gidx = lax.top_k(grp_score, TOPK_GROUPS)
    gmask = jnp.sum(jax.nn.one_hot(gidx, N_EXPERT_GROUPS, dtype=jnp.float32), axis=-2)
    emask = jnp.repeat(gmask, N_EXPERTS // N_EXPERT_GROUPS, axis=-1)
    sel = jnp.where(emask > 0, sel, -jnp.inf)
    _, eidx = lax.top_k(sel, TOP_K)
    wts = jnp.take_along_axis(scores, eidx, axis=-1)
    wts = wts / jnp.sum(wts, axis=-1, keepdims=True) * ROUTED_SCALE
    gates = jnp.sum(jax.nn.one_hot(eidx, N_EXPERTS, dtype=jnp.float32) * wts[..., None], axis=-2).astype(tok.dtype)

    def expert_step(acc, ew):
        w1e, w3e, w2e, ge = ew
        hid = jax.nn.silu(tok @ w1e) * (tok @ w3e)
        return acc + ge[:, None] * (hid @ w2e), None

    routed, _ = lax.scan(expert_step, jnp.zeros_like(tok), (w1, w3, w2, gates.T))
    shared = (jax.nn.silu(tok @ sw1) * (tok @ sw3)) @ sw2
    return routed + shared


def setup_inputs(seed: int = 0) -> dict:
    key = jax.random.key(seed)
    keys = iter(jax.random.split(key, 64))
    f32 = jnp.float32
    d = D_MODEL
    ne, no = (DEPTH + 1) // 2, DEPTH // 2

    def nrm(shape, scale):
        return jax.random.normal(next(keys), shape, f32) * scale

    def gain(shape):
        return 1.0 + nrm(shape, 0.02)

    def unif(shape, lo, hi):
        return jax.random.uniform(next(keys), shape, f32, lo, hi)

    dt0 = jnp.exp(unif((ne, 2, SSM_HEADS), math.log(1e-3), math.log(1e-1)))
    return {
        "x": nrm((BATCH, SEQ, d), 1.0),
        "c": nrm((BATCH, d), 1.0),
        "ctx": nrm((BATCH, CTX_LEN, d), 1.0),
        "c_ctx": nrm((d,), 1.0),
        "ada_w": nrm((DEPTH, d, 6 * d), 0.5 * d ** -0.5),
        "ada_b": nrm((DEPTH, 6 * d), 0.01),
        "norm_mix": gain((DEPTH, d)),
        "norm_ffn": gain((DEPTH, d)),
        "ev_w_in": nrm((ne, d, EVEN_COLS), d ** -0.5),
        "ev_w_out": nrm((ne, MIX_DIM, d), MIX_DIM ** -0.5),
        "rw_mu": unif((ne, RW_COLS), 0.0, 1.0),
        "rw_w0": unif((ne, 2, RW_DIM), -6.0, -1.0),
        "rw_w2": nrm((ne, 2, RW_DECAY_LORA, RW_DIM), 0.5 * RW_DECAY_LORA ** -0.5),
        "rw_a0": nrm((ne, 2, RW_DIM), 0.1),
        "rw_a2": nrm((ne, 2, RW_ICL_LORA, RW_DIM), RW_ICL_LORA ** -0.5),
        "rw_g2": nrm((ne, RW_GATE_LORA, RW_DIM), RW_GATE_LORA ** -0.5),
        "rw_kk": 0.85 + nrm((ne, RW_DIM), 0.02),
        "rw_ka": 1.0 + nrm((ne, RW_DIM), 0.02),
        "rw_rk": nrm((ne, RW_DIM), 0.1),
        "rw_gn_w": gain((ne, RW_DIM)),
        "rw_gn_b": nrm((ne, RW_DIM), 0.01),
        "ssm_conv_w": nrm((ne, SSM_CONV, XBC_DIM), SSM_CONV ** -0.5),
        "ssm_conv_b": nrm((ne, XBC_DIM), 0.01),
        "ssm_dt_bias": dt0 + jnp.log(-jnp.expm1(-dt0)),
        "ssm_a_log": jnp.log(unif((ne, 2, SSM_HEADS), 1.0, 16.0)),
        "ssm_d": gain((ne, SSM_HEADS)),
        "ssm_norm_w": gain((ne, SSM_DIM)),
        "mla_w_in": nrm((no, d, ODD_COLS), d ** -0.5),
        "mla_q_norm": gain((no, MLA_Q_LORA)),
        "mla_q_up": nrm((no, MLA_Q_LORA, MLA_HEADS * (MLA_NOPE + MLA_ROPE)), MLA_Q_LORA ** -0.5),
        "mla_kv_norm": gain((no, MLA_KV_LORA)),
        "mla_kv_up": nrm((no, MLA_KV_LORA, MLA_HEADS * (MLA_NOPE + MLA_V)), MLA_KV_LORA ** -0.5),
        "mla_w_out": nrm((no, MLA_HEADS * MLA_V, d), (MLA_HEADS * MLA_V) ** -0.5),
        "router_w": nrm((DEPTH, d, N_EXPERTS), d ** -0.5),
        "router_bias": nrm((DEPTH, N_EXPERTS), 0.01),
        "exp_w1": nrm((DEPTH, N_EXPERTS, d, EXPERT_FF), d ** -0.5),
        "exp_w3": nrm((DEPTH, N_EXPERTS, d, EXPERT_FF), d ** -0.5),
        "exp_w2": nrm((DEPTH, N_EXPERTS, EXPERT_FF, d), 0.5 * EXPERT_FF ** -0.5),
        "sh_w1": nrm((DEPTH, d, SHARED_FF), d ** -0.5),
        "sh_w3": nrm((DEPTH, d, SHARED_FF), d ** -0.5),
        "sh_w2": nrm((DEPTH, SHARED_FF, d), 0.5 * SHARED_FF ** -0.5),
        "final_norm": gain((d,)),
    }


def reference(x, c, ctx, c_ctx, ada_w, ada_b, norm_mix, norm_ffn,
              ev_w_in, ev_w_out, rw_mu, rw_w0, rw_w2, rw_a0, rw_a2, rw_g2, rw_kk, rw_ka, rw_rk,
              rw_gn_w, rw_gn_b, ssm_conv_w, ssm_conv_b, ssm_dt_bias, ssm_a_log, ssm_d, ssm_norm_w,
              mla_w_in, mla_q_norm, mla_q_up, mla_kv_norm, mla_kv_up, mla_w_out,
              router_w, router_bias, exp_w1, exp_w3, exp_w2, sh_w1, sh_w3, sh_w2, final_norm):
    cos, sin = axial_rope_tables(x.shape[1], x.dtype)
    xl, xc = x, ctx
    silu_c, silu_cc = jax.nn.silu(c), jax.nn.silu(c_ctx)
    for i in range(DEPTH):
        last = i == DEPTH - 1
        mod_l = jnp.split((silu_c @ ada_w[i] + ada_b[i])[:, None, :], 6, axis=-1)
        mod_c = jnp.split(silu_cc @ ada_w[i] + ada_b[i], 6, axis=-1)
        hl = modulate(rmsnorm(xl, norm_mix[i]), mod_l[0], mod_l[1])
        hc = modulate(rmsnorm(xc, norm_mix[i]), mod_c[0], mod_c[1])
        if i % 2 == 0:
            e = i // 2
            ol, oc = even_mixer(hl, hc, ev_w_in[e], ev_w_out[e], rw_mu[e], rw_w0[e], rw_w2[e], rw_a0[e],
                                rw_a2[e], rw_g2[e], rw_kk[e], rw_ka[e], rw_rk[e], rw_gn_w[e], rw_gn_b[e],
                                ssm_conv_w[e], ssm_conv_b[e], ssm_dt_bias[e], ssm_a_log[e], ssm_d[e],
                                ssm_norm_w[e])
        else:
            o = i // 2
            ol, oc = mla_mixer(hl, hc, mla_w_in[o], mla_q_norm[o], mla_q_up[o], mla_kv_norm[o],
                               mla_kv_up[o], mla_w_out[o], cos, sin, not last)
        xl = xl + mod_l[2] * ol
        hl = modulate(rmsnorm(xl, norm_ffn[i]), mod_l[3], mod_l[4])
        bsz, t, d = hl.shape
        if last:
            f_l = moe_ffn(hl.reshape(-1, d), router_w[i], router_bias[i], exp_w1[i], exp_w3[i], exp_w2[i],
                          sh_w1[i], sh_w3[i], sh_w2[i])
            xl = xl + mod_l[5] * f_l.reshape(bsz, t, d)
        else:
            xc = xc + mod_c[2] * oc
            hc = modulate(rmsnorm(xc, norm_ffn[i]), mod_c[3], mod_c[4])
            tok = jnp.concatenate([hl.reshape(-1, d), hc.reshape(-1, d)], axis=0)
            f = moe_ffn(tok, router_w[i], router_bias[i], exp_w1[i], exp_w3[i], exp_w2[i],
                        sh_w1[i], sh_w3[i], sh_w2[i])
            xl = xl + mod_l[5] * f[:bsz * t].reshape(bsz, t, d)
            xc = xc + mod_c[5] * f[bsz * t:].reshape(xc.shape)
    return rmsnorm(xl, final_norm)
```

```python
import functools
import math

import jax
import jax.numpy as jnp
import numpy as np
from jax import lax
from jax.experimental import pallas as pl
from jax.experimental.pallas import tpu as pltpu

F32 = jnp.float32
BF16 = jnp.bfloat16

D_MODEL = 1024
NORM_EPS = 1e-6
GRID_W = 64

RW_HEADS = 8
RW_HEAD = 64
RW_DIM = 512
RW_COLS = 1920
RW_GN_EPS = 64e-5

SSM_HEADS = 16
SSM_HEAD = 64
SSM_DIM = 1024
SSM_GROUPS = 2
SSM_STATE = 128
SSM_CONV = 5
XBC_DIM = 1536
MIX_DIM = 1536

MLA_HEADS = 16
MLA_Q_LORA = 384
MLA_KV_LORA = 256
MLA_NOPE = 64
MLA_ROPE = 32
MLA_V = 64
ROPE_THETA = 10000.0

N_EXPERTS = 64
TOP_K = 6
N_EXPERT_GROUPS = 8
TOPK_GROUPS = 4
EXPERT_FF = 256
ROUTED_SCALE = 2.5

SUBLANES = 8
LANES = 128
HALO = SUBLANES
VMEM_LIMIT = 56 * 2**20


class Geom:
    def __init__(self, b, t, c):
        self.b, self.t, self.c = b, t, c
        self.nl, self.nc = b * t, b * c
        self.n = self.nl + self.nc


def _row_tile(g, want, within_seq=False):
    tile = math.gcd(want, g.t)
    if within_seq:
        tile = math.gcd(tile, g.c)
    assert g.nc % tile == 0 and tile % SUBLANES == 0
    return tile


def _cparams(*sem):
    return pltpu.CompilerParams(dimension_semantics=sem, vmem_limit_bytes=VMEM_LIMIT)


def _full(shape):
    nd = len(shape)
    return pl.BlockSpec(shape, lambda *_: (0,) * nd)


def _bdot(a, b):
    return jnp.dot(a.astype(BF16), b.astype(BF16), preferred_element_type=F32)


def _split3(a):
    a1 = a.astype(BF16)
    r = a - a1.astype(F32)
    a2 = r.astype(BF16)
    r = r - a2.astype(F32)
    return a1, a2, r.astype(BF16)


def _xdot_l(a, m):
    return sum(jnp.dot(p, m, preferred_element_type=F32) for p in _split3(a))


def _xdot_r(m, a):
    return sum(jnp.dot(m, p, preferred_element_type=F32) for p in _split3(a))


def _sigmoid(x):
    return 1.0 / (1.0 + jnp.exp(-x))


def _silu(x):
    return x * _sigmoid(x)


def _softplus(x):
    return jnp.maximum(x, 0.0) + jnp.log(1.0 + jnp.exp(-jnp.abs(x)))


def _seq_pos(r, tile, g):
    n_lat = g.nl // tile
    per_l, per_c = g.t // tile, g.c // tile
    is_lat = r < n_lat
    pos = jnp.where(is_lat, r % per_l, (r - n_lat) % per_c)
    per = jnp.where(is_lat, per_l, per_c)
    return pos == 0, pos == per - 1


def _mod_spec(g, tile, col):
    return pl.BlockSpec((1, 1, D_MODEL), lambda r, *_: (jnp.minimum(r * tile // g.t, g.b), 0, col))


def _halo_specs(g, tile, width):
    nb = g.n // HALO
    per = tile // HALO
    prev = pl.BlockSpec((HALO, width), lambda r: (jnp.maximum(r * per - 1, 0), 0))
    nxt = pl.BlockSpec((HALO, width), lambda r: (jnp.minimum((r + 1) * per, nb - 1), 0))
    return prev, nxt


def _fill_halo(scr, cur_ref, prev_ref, next_ref, first, last, tile):
    scr[pl.ds(0, HALO), :] = prev_ref[...] * jnp.where(first, 0.0, 1.0)
    scr[pl.ds(HALO, tile), :] = cur_ref[...]
    scr[pl.ds(HALO + tile, HALO), :] = next_ref[...] * jnp.where(last, 0.0, 1.0)


def _mods_kernel(c_ref, w_ref, b_ref, o_ref):
    o_ref[0] = _bdot(_silu(c_ref[...]), w_ref[0]) + b_ref[0]


def ada_mods(cond, ada_w, ada_b):
    depth, d, n6 = ada_w.shape
    rows = cond.shape[0]
    tn = 1536
    return pl.pallas_call(
        _mods_kernel,
        grid=(depth, n6 // tn),
        in_specs=[pl.BlockSpec((rows, d), lambda l, j: (0, 0)),
                  pl.BlockSpec((1, d, tn), lambda l, j: (l, 0, j)),
                  pl.BlockSpec((1, 1, tn), lambda l, j: (l, 0, j))],
        out_specs=pl.BlockSpec((1, rows, tn), lambda l, j: (l, 0, j)),
        out_shape=jax.ShapeDtypeStruct((depth, rows, n6), F32),
        compiler_params=_cparams("arbitrary", "arbitrary"),
        name="ada_mods",
    )(cond, ada_w, ada_b.reshape(depth, 1, n6))


def _normmod(x, gain, shift, scale):
    ms = jnp.mean(x * x, axis=-1, keepdims=True)
    h = x * lax.rsqrt(ms + NORM_EPS) * gain
    return h * (1.0 + scale) + shift


def _normmod_mm_kernel(x_ref, g_ref, sh_ref, sc_ref, *refs):
    nw = len(refs) // 2
    h = _normmod(x_ref[...], g_ref[...], sh_ref[0], sc_ref[0]).astype(BF16)
    for w_ref, o_ref in zip(refs[:nw], refs[nw:]):
        o_ref[...] = jnp.dot(h, w_ref[...], preferred_element_type=F32).astype(o_ref.dtype)


def normmod_mm(g, xa, nrows, gain, mods, shift_col, scale_col, weights, tile=256):
    d = xa.shape[1]
    tile = _row_tile(g, tile)
    return pl.pallas_call(
        _normmod_mm_kernel,
        grid=(nrows // tile,),
        in_specs=[pl.BlockSpec((tile, d), lambda r: (r, 0)), _full((1, d)),
                  _mod_spec(g, tile, shift_col), _mod_spec(g, tile, scale_col)]
                 + [_full(w.shape) for w in weights],
        out_specs=[pl.BlockSpec((tile, w.shape[1]), lambda r: (r, 0)) for w in weights],
        out_shape=[jax.ShapeDtypeStruct((nrows, w.shape[1]), F32) for w in weights],
        compiler_params=_cparams("arbitrary"),
        name="normmod_mm",
    )(xa, gain.reshape(1, d), mods, mods, *weights)


def _head_ones():
    i = np.arange(RW_DIM) // RW_HEAD
    return jnp.asarray(i[:, None] == i[None, :], BF16)


def _rw_prep_kernel(g, tile, cur_ref, prev_ref, next_ref, mu_ref, w0_ref, w2_ref, a0_ref, a2_ref, g2_ref,
                    kk_ref, ones_ref, r_out, k_out, v_out, kk_out, w_out, a_out, g_out, scr):
    first, last = _seq_pos(pl.program_id(0), tile, g)
    _fill_halo(scr, cur_ref, prev_ref, next_ref, first, last, tile)
    x = cur_ref[...]
    nb = 0.5 * (scr[pl.ds(HALO - 1, tile), :] + scr[pl.ds(HALO + 1, tile), :])
    mixed = x + mu_ref[...] * (nb - x)
    r = mixed[:, 0:512]
    k = mixed[:, 512:1024]
    v = mixed[:, 1024:1536]
    xw = mixed[:, 1536:1664]
    xa = mixed[:, 1664:1792]
    xg = mixed[:, 1792:1920]
    w_log = -_softplus(-(w0_ref[...] + _bdot(jnp.tanh(xw), w2_ref[...]))) - 0.5
    w_out[...] = jnp.exp(-jnp.exp(w_log))
    a_out[...] = _sigmoid(a0_ref[...] + _bdot(xa, a2_ref[...]))
    g_out[...] = _bdot(_sigmoid(xg), g2_ref[...])
    kkr = k * kk_ref[...]
    ss = _xdot_l(kkr * kkr, ones_ref[...])
    r_out[...] = r
    k_out[...] = k
    v_out[...] = v
    kk_out[...] = kkr * lax.rsqrt(ss + 1e-12)


def _blockdiag2(w):
    z = jnp.zeros_like(w[0])
    return jnp.concatenate([jnp.concatenate([w[0], z], 1), jnp.concatenate([z, w[1]], 1)], 0)


def rw_prep(g, rw, mu, w0, w2, a0, a2, g2, k_k, tile):
    n = g.n
    prev, nxt = _halo_specs(g, tile, RW_COLS)
    row = lambda w: pl.BlockSpec((tile, w), lambda r: (r, 0))
    shp = lambda w: jax.ShapeDtypeStruct((n, w), F32)
    return pl.pallas_call(
        functools.partial(_rw_prep_kernel, g, tile),
        grid=(n // tile,),
        in_specs=[row(RW_COLS), prev, nxt, _full((1, RW_COLS)), _full((1, 1024)), _full((128, 1024)),
                  _full((1, 1024)), _full((128, 1024)), _full((128, 512)), _full((1, 512)), _full((512, 512))],
        out_specs=[row(512), row(512), row(512), row(512), row(1024), row(1024), row(512)],
        out_shape=[shp(512), shp(512), shp(512), shp(512), shp(1024), shp(1024), shp(512)],
        scratch_shapes=[pltpu.VMEM((tile + 2 * HALO, RW_COLS), F32)],
        compiler_params=_cparams("arbitrary"),
        name="rw_prep",
    )(rw, rw, rw, mu.reshape(1, -1), w0.reshape(1, -1), _blockdiag2(w2).astype(BF16), a0.reshape(1, -1),
      _blockdiag2(a2).astype(BF16), g2.astype(BF16), k_k.reshape(1, -1), _head_ones())


def _rw_scan_kernel(tt, r_ref, k_ref, v_ref, kk_ref, w_ref, a_ref, ka_ref, y_ref, s_scr, b_scr, kd_scr):
    d = pl.program_id(0)

    @pl.when(pl.program_id(2) == 0)
    def _():
        s_scr[...] = jnp.zeros_like(s_scr)

    a = a_ref[0]
    b_scr[...] = kk_ref[...] * a
    kd_scr[...] = k_ref[...] * (1.0 + (a - 1.0) * ka_ref[...])
    nv = RW_HEAD // SUBLANES

    def step(i, carry):
        t = i + d * (tt - 1 - 2 * i)
        vt = [v_ref[t, pl.ds(j * SUBLANES, SUBLANES), :] for j in range(nv)]
        sa = [jnp.zeros((SUBLANES, LANES), F32) for _ in range(nv)]
        for k in range(RW_HEAD):
            kkb = kk_ref[t, pl.ds(k, 1), :]
            for j in range(nv):
                sa[j] = sa[j] + s_scr[k, pl.ds(j * SUBLANES, SUBLANES), :] * kkb
        y = [jnp.zeros((SUBLANES, LANES), F32) for _ in range(nv)]
        for k in range(RW_HEAD):
            wb = w_ref[0, t, pl.ds(k, 1), :]
            bb = b_scr[t, pl.ds(k, 1), :]
            kb = kd_scr[t, pl.ds(k, 1), :]
            rb = r_ref[t, pl.ds(k, 1), :]
            for j in range(nv):
                sl = pl.ds(j * SUBLANES, SUBLANES)
                new = s_scr[k, sl, :] * wb + (vt[j] * kb - sa[j] * bb)
                s_scr[k, sl, :] = new
                y[j] = y[j] + new * rb
        for j in range(nv):
            y_ref[0, t, pl.ds(j * SUBLANES, SUBLANES), :] = y[j]
        return carry

    lax.fori_loop(0, tt, step, 0)


def rw_scan(r, k, v, kk, w, a, ka, c_len, tt):
    s_len, _, chains = r.shape
    nb, ncb = s_len // tt, c_len // tt

    def tb(d, s):
        bwd = jnp.where(s < ncb, ncb - 1 - s, nb - 1 - (s - ncb))
        return jnp.where(d == 0, s, bwd)

    shared = pl.BlockSpec((tt, RW_HEAD, LANES), lambda d, c, s: (tb(d, s), 0, c))
    per_dir = pl.BlockSpec((1, tt, RW_HEAD, LANES), lambda d, c, s: (d, tb(d, s), 0, c))
    return pl.pallas_call(
        functools.partial(_rw_scan_kernel, tt),
        grid=(2, chains // LANES, nb),
        in_specs=[shared, shared, shared, shared, per_dir, per_dir,
                  pl.BlockSpec((RW_HEAD, LANES), lambda d, c, s: (0, c))],
        out_specs=per_dir,
        out_shape=jax.ShapeDtypeStruct((2, s_len, RW_HEAD, chains), F32),
        scratch_shapes=[pltpu.VMEM((RW_HEAD, RW_HEAD, LANES), F32), pltpu.VMEM((tt, RW_HEAD, LANES), F32),
                        pltpu.VMEM((tt, RW_HEAD, LANES), F32)],
        compiler_params=_cparams("arbitrary", "arbitrary", "arbitrary"),
        name="rw_scan",
    )(r, k, v, kk, w, a, ka)


def _to_chains(g, u):
    h = u.shape[1] // RW_HEAD
    ul = u[:g.nl].reshape(g.b, g.t, h, RW_HEAD)
    uc = u[g.nl:].reshape(g.b, g.c, h, RW_HEAD)
    full = jnp.concatenate([uc, ul], axis=1)
    return full.transpose(1, 3, 0, 2).reshape(g.c + g.t, RW_HEAD, g.b * h)


def _from_chains(g, y):
    h = y.shape[2] // g.b
    y = y.reshape(g.c + g.t, RW_HEAD, g.b, h).transpose(2, 0, 3, 1).reshape(g.b, g.c + g.t, h * RW_HEAD)
    return jnp.concatenate([y[:, g.c:].reshape(g.nl, -1), y[:, :g.c].reshape(g.nc, -1)], axis=0)


def _conv_kernel(g, tile, cur_ref, prev_ref, next_ref, w_ref, b_ref, o_ref, scr):
    first, last = _seq_pos(pl.program_id(0), tile, g)
    _fill_halo(scr, cur_ref, prev_ref, next_ref, first, last, tile)
    acc = b_ref[...] + w_ref[pl.ds(0, 1), :] * scr[pl.ds(HALO - 2, tile), :]
    for j in range(1, SSM_CONV):
        acc = acc + w_ref[pl.ds(j, 1), :] * scr[pl.ds(HALO - 2 + j, tile), :]
    o_ref[...] = _silu(acc)


def ssm_conv(g, xbc, conv_w, conv_b, tile):
    prev, nxt = _halo_specs(g, tile, XBC_DIM)
    row = pl.BlockSpec((tile, XBC_DIM), lambda r: (r, 0))
    return pl.pallas_call(
        functools.partial(_conv_kernel, g, tile),
        grid=(g.n // tile,),
        in_specs=[row, prev, nxt, _full((SSM_CONV, XBC_DIM)), _full((1, XBC_DIM))],
        out_specs=row,
        out_shape=jax.ShapeDtypeStruct((g.n, XBC_DIM), F32),
        scratch_shapes=[pltpu.VMEM((tile + 2 * HALO, XBC_DIM), F32)],
        compiler_params=_cparams("arbitrary"),
        name="ssm_conv",
    )(xbc, xbc, xbc, conv_w, conv_b.reshape(1, -1))


def _ssd_dir(rev, xbc, dtr, bias, aneg, expand, tri, st_ref):
    cl = xbc.shape[0]
    xs = xbc[:, 0:SSM_DIM]
    dt = _softplus(dtr + bias)
    da = dt * aneg
    cs = _xdot_r(tri, da)
    dt_x = _xdot_l(dt, expand)
    cs_x = _xdot_l(cs, expand)
    cs_end = cs_x[0:1, :] if rev else cs_x[cl - 1:cl, :]
    xdt = xs * dt_x
    cs_t = cs.T
    row = lax.broadcasted_iota(jnp.int32, (cl, cl), 0)
    col = lax.broadcasted_iota(jnp.int32, (cl, cl), 1)
    keep = (col >= row) if rev else (col <= row)
    lane = lax.broadcasted_iota(jnp.int32, (cl, LANES), 1)
    per_group = SSM_DIM // SSM_GROUPS
    y_parts = []
    for gi in range(SSM_GROUPS):
        bm = xbc[:, SSM_DIM + gi * SSM_STATE:SSM_DIM + (gi + 1) * SSM_STATE]
        cm = xbc[:, SSM_DIM + SSM_GROUPS * SSM_STATE + gi * SSM_STATE:
                 SSM_DIM + SSM_GROUPS * SSM_STATE + (gi + 1) * SSM_STATE]
        cb = lax.dot_general(cm.astype(BF16), bm.astype(BF16), (((1,), (1,)), ((), ())),
                             preferred_element_type=F32)
        gs = slice(gi * per_group, (gi + 1) * per_group)
        s_in = st_ref[:, gs]
        y_off = _bdot(cm, s_in) * jnp.exp(cs_x[:, gs])
        for p in range(per_group // LANES):
            h0 = (gi * per_group + p * LANES) // SSM_HEAD
            x_pair = xdt[:, h0 * SSM_HEAD:h0 * SSM_HEAD + LANES].astype(BF16)
            ys = []
            for hh in (h0, h0 + 1):
                seg = cs[:, hh:hh + 1] - cs_t[hh:hh + 1, :]
                m = cb * jnp.exp(jnp.where(keep, seg, -jnp.inf))
                ys.append(jnp.dot(m.astype(BF16), x_pair, preferred_element_type=F32))
            y_parts.append(jnp.where(lane < SSM_HEAD, ys[0], ys[1]) + y_off[:, p * LANES:(p + 1) * LANES])
        xd = (xdt[:, gs] * jnp.exp(cs_end[:, gs] - cs_x[:, gs])).astype(BF16)
        st_new = lax.dot_general(bm.astype(BF16), xd, (((0,), (0,)), ((), ())), preferred_element_type=F32)
        st_ref[:, gs] = s_in * jnp.exp(cs_end[:, gs]) + st_new
    return jnp.concatenate(y_parts, axis=1)


def _ssd_kernel(xf_ref, dtf_ref, xb_ref, dtb_ref, bias_ref, aneg_ref, exp_ref, tril_ref, triu_ref,
                yf_ref, yb_ref, st_scr):
    @pl.when(pl.program_id(1) == 0)
    def _():
        st_scr[...] = jnp.zeros_like(st_scr)

    yf_ref[...] = _ssd_dir(False, xf_ref[...], dtf_ref[...], bias_ref[0], aneg_ref[0], exp_ref[...],
                           tril_ref[...], st_scr.at[0])
    yb_ref[...] = _ssd_dir(True, xb_ref[...], dtb_ref[...], bias_ref[1], aneg_ref[1], exp_ref[...],
                           triu_ref[...], st_scr.at[1])


def ssd_scan(g, xbc_act, dt_f, dt_b, dt_bias, a_log, cl):
    ncc, nlc = g.c // cl, g.t // cl
    ns = ncc + nlc

    def rows(b, c):
        return jnp.where(c < ncc, g.nl // cl + b * ncc + c, b * nlc + (c - ncc))

    def fwd(b, s):
        return rows(b, s)

    def bwd(b, s):
        return rows(b, jnp.where(s < ncc, ncc - 1 - s, ns - 1 - (s - ncc)))

    pad = lambda u: jnp.pad(u, ((0, 0), (0, LANES - SSM_HEADS))).reshape(2, 1, LANES)
    expand = jnp.asarray(np.arange(LANES)[:, None] == (np.arange(SSM_DIM) // SSM_HEAD)[None, :], BF16)
    idx = np.arange(cl)
    tril = jnp.asarray(idx[:, None] >= idx[None, :], BF16)
    triu = jnp.asarray(idx[:, None] <= idx[None, :], BF16)
    spec = lambda w, f: pl.BlockSpec((cl, w), lambda b, s: (f(b, s), 0))
    return pl.pallas_call(
        _ssd_kernel,
        grid=(g.b, ns),
        in_specs=[spec(XBC_DIM, fwd), spec(LANES, fwd), spec(XBC_DIM, bwd), spec(LANES, bwd),
                  _full((2, 1, LANES)), _full((2, 1, LANES)), _full((LANES, SSM_DIM)), _full((cl, cl)),
                  _full((cl, cl))],
        out_specs=[spec(SSM_DIM, fwd), spec(SSM_DIM, bwd)],
        out_shape=[jax.ShapeDtypeStruct((g.n, SSM_DIM), F32)] * 2,
        scratch_shapes=[pltpu.VMEM((2, SSM_STATE, SSM_DIM), F32)],
        compiler_params=_cparams("arbitrary", "arbitrary"),
        name="ssd_scan",
    )(xbc_act, dt_f, xbc_act, dt_b, pad(dt_bias), pad(-jnp.exp(a_log)), expand, tril, triu)


def _even_out_kernel(x_ref, gate_ref, yrf_ref, yrb_ref, r_ref, k_ref, v_ref, a_ref, g_ref,
                     ysf_ref, ysb_ref, xs_ref, z_ref, ka_ref, rk_ref, gnw_ref, gnb_ref, ones_ref,
                     dsk_ref, nw_ref, wo_rw_ref, wo_ss_ref, o_ref):
    ones = ones_ref[...]
    y = yrf_ref[...] + yrb_ref[...]
    mean = _xdot_l(y, ones) * (1.0 / RW_HEAD)
    dev = y - mean
    var = _xdot_l(dev * dev, ones) * (1.0 / RW_HEAD)
    yn = dev * lax.rsqrt(var + RW_GN_EPS) * gnw_ref[...] + gnb_ref[...]
    r, k, a, ka = r_ref[...], k_ref[...], a_ref[...], ka_ref[...]
    ksum = k * (1.0 + (a[:, 0:RW_DIM] - 1.0) * ka) + k * (1.0 + (a[:, RW_DIM:] - 1.0) * ka)
    bonus = _xdot_l(r * ksum * rk_ref[...], ones) * v_ref[...]
    out_rw = (yn + bonus) * g_ref[...]

    ys = ysf_ref[...] + ysb_ref[...] + xs_ref[...] * dsk_ref[...]
    ys = ys * _silu(z_ref[...])
    half = SSM_DIM // SSM_GROUPS
    parts = []
    for gi in range(SSM_GROUPS):
        yg = ys[:, gi * half:(gi + 1) * half]
        parts.append(yg * lax.rsqrt(jnp.mean(yg * yg, axis=-1, keepdims=True) + NORM_EPS))
    out_ss = jnp.concatenate(parts, axis=1) * nw_ref[...]
    o = _bdot(out_rw, wo_rw_ref[...]) + _bdot(out_ss, wo_ss_ref[...])
    o_ref[...] = x_ref[...] + gate_ref[0] * o


def even_out(g, xa, mods, yr_f, yr_b, r, k, v, a, gg, ys_f, ys_b, xbc_act, z, ka, rk, gn_w, gn_b,
             d_skip, norm_w, w_out, tile=256):
    tile = _row_tile(g, tile)
    row = lambda w: pl.BlockSpec((tile, w), lambda i: (i, 0))
    vec = lambda w: _full((1, w))
    return pl.pallas_call(
        _even_out_kernel,
        grid=(g.n // tile,),
        in_specs=[row(D_MODEL), _mod_spec(g, tile, 2), row(512), row(512), row(512), row(512), row(512),
                  row(1024), row(512), row(1024), row(1024), row(1024), row(1024),
                  vec(512), vec(512), vec(512), vec(512), _full((512, 512)), vec(1024), vec(1024),
                  _full((RW_DIM, D_MODEL)), _full((SSM_DIM, D_MODEL))],
        out_specs=row(D_MODEL),
        out_shape=jax.ShapeDtypeStruct((g.n, D_MODEL), F32),
        compiler_params=_cparams("arbitrary"),
        name="even_out",
    )(xa, mods, yr_f, yr_b, r, k, v, a, gg, ys_f, ys_b, xbc_act, z, ka.reshape(1, -1), rk.reshape(1, -1),
      gn_w.reshape(1, -1), gn_b.reshape(1, -1), _head_ones(), jnp.repeat(d_skip, SSM_HEAD).reshape(1, -1),
      norm_w.reshape(1, -1), w_out[:RW_DIM].astype(BF16), w_out[RW_DIM:].astype(BF16))


def even_layer(g, xa, mods, norm_g, w_in, w_out, mu, w0, w2, a0, a2, g2, k_k, k_a, r_k, gn_w, gn_b,
               conv_w, conv_b, dt_bias, a_log, d_skip, ssm_norm_w):
    o_z, o_x, o_dt = RW_COLS, RW_COLS + SSM_DIM, RW_COLS + SSM_DIM + XBC_DIM
    padw = lambda w: jnp.pad(w, ((0, 0), (0, LANES - w.shape[1])))
    weights = [w_in[:, :o_z], w_in[:, o_z:o_x], w_in[:, o_x:o_dt], padw(w_in[:, o_dt:o_dt + SSM_HEADS]),
               padw(w_in[:, o_dt + SSM_HEADS:])]
    rw, z, xbc, dt_f, dt_b = normmod_mm(g, xa, g.n, norm_g, mods, 0, 1, [w.astype(BF16) for w in weights])

    seq_tile = _row_tile(g, 256, within_seq=True)
    r, k, v, kk, w, a, gg = rw_prep(g, rw, mu, w0, w2, a0, a2, g2, k_k, seq_tile)
    tc = functools.partial(_to_chains, g)
    both = lambda u: jnp.stack([tc(u[:, :RW_DIM]), tc(u[:, RW_DIM:])])
    ka_c = jnp.tile(k_a.reshape(RW_HEADS, RW_HEAD).T, (1, g.b))
    y = rw_scan(tc(r), tc(k), tc(v), tc(kk), both(w), both(a), ka_c, g.c, math.gcd(32, g.c))
    yr_f, yr_b = _from_chains(g, y[0]), _from_chains(g, y[1])

    xbc_act = ssm_conv(g, xbc, conv_w, conv_b, seq_tile)
    ys_f, ys_b = ssd_scan(g, xbc_act, dt_f, dt_b, dt_bias, a_log, math.gcd(128, g.c))
    return even_out(g, xa, mods, yr_f, yr_b, r, k, v, a, gg, ys_f, ys_b, xbc_act, z, k_a, r_k, gn_w, gn_b,
                    d_skip, ssm_norm_w, w_out)


def _router_kernel(x_ref, g_ref, sh_ref, sc_ref, rw_ref, rb_ref, h_ref, gates_ref):
    h = _normmod(x_ref[...], g_ref[...], sh_ref[0], sc_ref[0])
    h_ref[...] = h.astype(BF16)
    nt = (((1,), (1,)), ((), ()))
    h3, w3 = _split3(h), _split3(rw_ref[...])
    logits = sum(lax.dot_general(w3[i], h3[j], nt, preferred_element_type=F32)
                 for i in range(3) for j in range(3 - i))
    scores = _sigmoid(logits)
    sel = scores + rb_ref[...]
    tile = sel.shape[1]
    per = N_EXPERTS // N_EXPERT_GROUPS
    grp = sel.reshape(N_EXPERT_GROUPS, per, tile)
    pos = lax.broadcasted_iota(jnp.int32, grp.shape, 1).astype(F32)
    m1 = jnp.max(grp, axis=1, keepdims=True)
    first = jnp.min(jnp.where(grp == m1, pos, float(per)), axis=1, keepdims=True)
    m2 = jnp.max(jnp.where(pos == first, -jnp.inf, grp), axis=1, keepdims=True)
    gsc = jnp.broadcast_to(m1 + m2, grp.shape)
    gid = lax.broadcasted_iota(jnp.int32, grp.shape, 0)
    beaten = jnp.zeros(grp.shape, F32)
    for o in range(N_EXPERT_GROUPS):
        other = gsc[o:o + 1]
        beaten = beaten + jnp.where((other > gsc) | ((other == gsc) & (o < gid)), 1.0, 0.0)
    cand = jnp.where(beaten < TOPK_GROUPS, grp, -jnp.inf).reshape(N_EXPERTS, tile)
    eid = lax.broadcasted_iota(jnp.int32, cand.shape, 0).astype(F32)
    chosen = jnp.zeros(cand.shape, F32)
    for _ in range(TOP_K):
        best = jnp.max(cand, axis=0, keepdims=True)
        pick = eid == jnp.min(jnp.where(cand == best, eid, float(N_EXPERTS)), axis=0, keepdims=True)
        chosen = jnp.where(pick, 1.0, chosen)
        cand = jnp.where(pick, -jnp.inf, cand)
    wts = scores * chosen
    gates = wts / jnp.sum(wts, axis=0, keepdims=True) * ROUTED_SCALE
    gates_ref[...] = jnp.concatenate([gates, jnp.zeros_like(gates)], axis=0).T


def moe_router(g, xa, nrows, gain, mods, router_w, router_bias, tile=512):
    d = xa.shape[1]
    tile = _row_tile(g, tile)
    return pl.pallas_call(
        _router_kernel,
        grid=(nrows // tile,),
        in_specs=[pl.BlockSpec((tile, d), lambda r: (r, 0)), _full((1, d)), _mod_spec(g, tile, 3),
                  _mod_spec(g, tile, 4), _full((N_EXPERTS, d)), _full((N_EXPERTS, 1))],
        out_specs=[pl.BlockSpec((tile, d), lambda r: (r, 0)), pl.BlockSpec((tile, LANES), lambda r: (r, 0))],
        out_shape=[jax.ShapeDtypeStruct((nrows, d), BF16), jax.ShapeDtypeStruct((nrows, LANES), F32)],
        compiler_params=_cparams("arbitrary"),
        name="moe_router",
    )(xa, gain.reshape(1, d), mods, mods, router_w.T, router_bias.reshape(N_EXPERTS, 1))


EXPERTS_PER_STEP = 4


def _experts_kernel(final, h_ref, gates_ref, w13_ref, w2_ref, sw13_ref, sw2_ref, x_ref, gate_ref, fin_ref,
                    o_ref, acc):
    e = pl.program_id(1)
    h = h_ref[...]
    ff = EXPERT_FF

    def ffn(w13):
        u = jnp.dot(h, w13, preferred_element_type=F32)
        return _silu(u[:, :ff]) * u[:, ff:]

    @pl.when(e == 0)
    def _():
        acc[...] = jnp.dot(ffn(sw13_ref[...]).astype(BF16), sw2_ref[...], preferred_element_type=F32)

    width = EXPERTS_PER_STEP * ff
    hot = (lax.broadcasted_iota(jnp.int32, (LANES, width), 0)
           == e * EXPERTS_PER_STEP + jnp.right_shift(lax.broadcasted_iota(jnp.int32, (LANES, width), 1),
                                                     int(math.log2(ff))))
    gcols = _xdot_l(gates_ref[...], hot.astype(BF16))
    hid = jnp.concatenate([ffn(w13_ref[j]) for j in range(EXPERTS_PER_STEP)], axis=1) * gcols
    acc[...] += jnp.dot(hid.astype(BF16), w2_ref[0], preferred_element_type=F32)

    @pl.when(e == pl.num_programs(1) - 1)
    def _():
        out = x_ref[...] + gate_ref[0] * acc[...]
        if final:
            ms = jnp.mean(out * out, axis=-1, keepdims=True)
            out = out * lax.rsqrt(ms + NORM_EPS) * fin_ref[...]
        o_ref[...] = out


def moe_experts(g, xa, nrows, mods, h, gates, w1, w3, w2, sw1, sw3, sw2, final_gain, final, tile=1024):
    d = xa.shape[1]
    tile = _row_tile(g, tile)
    eps = EXPERTS_PER_STEP
    w13 = jnp.concatenate([w1, w3], axis=2).astype(BF16)
    w2c = w2.astype(BF16).reshape(N_EXPERTS // eps, eps * EXPERT_FF, d)
    sw13 = jnp.concatenate([sw1, sw3], axis=1).astype(BF16)
    return pl.pallas_call(
        functools.partial(_experts_kernel, final),
        grid=(nrows // tile, N_EXPERTS // eps),
        in_specs=[pl.BlockSpec((tile, d), lambda i, e: (i, 0)),
                  pl.BlockSpec((tile, LANES), lambda i, e: (i, 0)),
                  pl.BlockSpec((eps, d, 2 * EXPERT_FF), lambda i, e: (e, 0, 0)),
                  pl.BlockSpec((1, eps * EXPERT_FF, d), lambda i, e: (e, 0, 0)),
                  _full(sw13.shape), _full(sw2.shape),
                  pl.BlockSpec((tile, d), lambda i, e: (i, 0)), _mod_spec(g, tile, 5), _full((1, d))],
        out_specs=pl.BlockSpec((tile, d), lambda i, e: (i, 0)),
        out_shape=jax.ShapeDtypeStruct((nrows, d), F32),
        scratch_shapes=[pltpu.VMEM((tile, d), F32)],
        compiler_params=_cparams("arbitrary", "arbitrary"),
        name="moe_experts",
    )(h, gates, w13, w2c, sw13, sw2.astype(BF16), xa, mods, final_gain.reshape(1, d))


def moe_layer(g, xa, nrows, mods, gain, router_w, router_bias, w1, w3, w2, sw1, sw3, sw2, final_gain, final):
    h, gates = moe_router(g, xa, nrows, gain, mods, router_w, router_bias)
    return moe_experts(g, xa, nrows, mods, h, gates, w1, w3, w2, sw1, sw3, sw2, final_gain, final)


HEAD_PAD = 128


def _rope_tables(g, scale):
    rows = g.t // GRID_W
    r_idx = jnp.repeat(jnp.arange(rows, dtype=F32), GRID_W)
    c_idx = jnp.tile(jnp.arange(GRID_W, dtype=F32), rows)
    axis_dim = MLA_ROPE // 2
    inv_freq = ROPE_THETA ** (-jnp.arange(0, axis_dim, 2, dtype=F32) / axis_dim)
    ang = jnp.concatenate([r_idx[:, None] * inv_freq, c_idx[:, None] * inv_freq], axis=-1)
    cos, sin = jnp.cos(ang), jnp.sin(ang)
    cos2 = jnp.concatenate([cos, cos], axis=1)
    sin2 = jnp.concatenate([-sin, sin], axis=1)
    t = g.t
    q_cos = jnp.concatenate([jnp.ones((t, MLA_NOPE), F32), cos2, jnp.zeros((t, 32), F32)], axis=1) * scale
    q_sin = jnp.concatenate([jnp.zeros((t, MLA_NOPE), F32), sin2, jnp.zeros((t, 32), F32)], axis=1) * scale
    zpad = jnp.zeros((t, LANES - MLA_ROPE), F32)
    k_cos = jnp.concatenate([cos2, zpad], axis=1)
    k_sin = jnp.concatenate([sin2, zpad], axis=1)
    c_cos = jnp.concatenate([jnp.ones((g.c, MLA_ROPE), F32), jnp.zeros((g.c, LANES - MLA_ROPE), F32)], axis=1)
    k_cos = jnp.concatenate([k_cos, c_cos], axis=0)
    k_sin = jnp.concatenate([k_sin, jnp.zeros((g.c, LANES), F32)], axis=0)
    return q_cos, q_sin, k_cos, k_sin


def _q_kernel(cq_ref, qn_ref, w1_ref, w2_ref, cos_ref, sin_ref, q_ref):
    cq = cq_ref[...]
    qn = (cq * lax.rsqrt(jnp.mean(cq * cq, axis=-1, keepdims=True) + NORM_EPS) * qn_ref[...]).astype(BF16)
    q1 = jnp.dot(qn, w1_ref[...], preferred_element_type=F32)
    q2 = jnp.dot(qn, w2_ref[...], preferred_element_type=F32)
    cos, sin = cos_ref[...], sin_ref[...]
    for h in range(MLA_HEADS):
        sl = slice(h * HEAD_PAD, (h + 1) * HEAD_PAD)
        q_ref[:, sl] = (q1[:, sl] * cos + q2[:, sl] * sin).astype(BF16)


def mla_queries(g, cq, q_norm, w1, w2, q_cos, q_sin, tile):
    per = g.t // tile
    width = MLA_HEADS * HEAD_PAD
    return pl.pallas_call(
        _q_kernel,
        grid=(g.nl // tile,),
        in_specs=[pl.BlockSpec((tile, MLA_Q_LORA), lambda r: (r, 0)), _full((1, MLA_Q_LORA)),
                  _full(w1.shape), _full(w2.shape),
                  pl.BlockSpec((tile, HEAD_PAD), lambda r: (r % per, 0)),
                  pl.BlockSpec((tile, HEAD_PAD), lambda r: (r % per, 0))],
        out_specs=pl.BlockSpec((tile, width), lambda r: (r, 0)),
        out_shape=jax.ShapeDtypeStruct((g.nl, width), BF16),
        compiler_params=_cparams("arbitrary"),
        name="mla_queries",
    )(cq, q_norm.reshape(1, -1), w1, w2, q_cos, q_sin)


def _kv_kernel(ckv_ref, kr1_ref, kr2_ref, kvn_ref, wk_ref, wv_ref, place_ref, cos_ref, sin_ref, k_ref, v_ref):
    ckv = ckv_ref[...]
    kvn = (ckv * lax.rsqrt(jnp.mean(ckv * ckv, axis=-1, keepdims=True) + NORM_EPS) * kvn_ref[...]).astype(BF16)
    kr = (kr1_ref[...] * cos_ref[...] + kr2_ref[...] * sin_ref[...]).astype(BF16)
    k = jnp.dot(kvn, wk_ref[...], preferred_element_type=F32) + jnp.dot(kr, place_ref[...], preferred_element_type=F32)
    k_ref[...] = k.astype(BF16)
    v_ref[...] = jnp.dot(kvn, wv_ref[...], preferred_element_type=F32).astype(BF16)


def mla_keys_values(g, ckv, kr1, kr2, kv_norm, wk, wv, place, k_cos, k_sin, tile):
    n_lat, per_l, per_c = g.nl // tile, g.t // tile, g.c // tile

    def tab(r):
        return (jnp.where(r < n_lat, r % per_l, per_l + (r - n_lat) % per_c), 0)

    row = lambda w: pl.BlockSpec((tile, w), lambda r: (r, 0))
    kw = MLA_HEADS * HEAD_PAD
    return pl.pallas_call(
        _kv_kernel,
        grid=(g.n // tile,),
        in_specs=[row(MLA_KV_LORA), row(LANES), row(LANES), _full((1, MLA_KV_LORA)), _full(wk.shape),
                  _full(wv.shape), _full(place.shape), pl.BlockSpec((tile, LANES), tab),
                  pl.BlockSpec((tile, LANES), tab)],
        out_specs=[row(kw), row(MLA_HEADS * MLA_V)],
        out_shape=[jax.ShapeDtypeStruct((g.n, kw), BF16), jax.ShapeDtypeStruct((g.n, MLA_HEADS * MLA_V), BF16)],
        compiler_params=_cparams("arbitrary"),
        name="mla_keys_values",
    )(ckv, kr1, kr2, kv_norm.reshape(1, -1), wk, wv, place, k_cos, k_sin)


def _attn_kernel(q_ref, kl_ref, kc_ref, vl_ref, vc_ref, o_ref):
    nt = (((1,), (1,)), ((), ()))
    vl, vc = vl_ref[...], vc_ref[...]
    outs = []
    for hh in range(2):
        sl = slice(hh * HEAD_PAD, (hh + 1) * HEAD_PAD)
        q = q_ref[:, sl]
        s_l = lax.dot_general(q, kl_ref[:, sl], nt, preferred_element_type=F32)
        s_c = lax.dot_general(q, kc_ref[:, sl], nt, preferred_element_type=F32)
        m = jnp.maximum(jnp.max(s_l, axis=-1, keepdims=True), jnp.max(s_c, axis=-1, keepdims=True))
        p_l, p_c = jnp.exp(s_l - m), jnp.exp(s_c - m)
        den = jnp.sum(p_l, axis=-1, keepdims=True) + jnp.sum(p_c, axis=-1, keepdims=True)
        o = (jnp.dot(p_l.astype(BF16), vl, preferred_element_type=F32)
             + jnp.dot(p_c.astype(BF16), vc, preferred_element_type=F32))
        outs.append(o / den)
    lane = lax.broadcasted_iota(jnp.int32, outs[0].shape, 1)
    o_ref[...] = jnp.where(lane < MLA_V, outs[0], outs[1]).astype(o_ref.dtype)


def mla_attention(g, q, k, v, tq):
    nq = g.t // tq
    pairs = MLA_HEADS // 2
    pw = 2 * HEAD_PAD
    ctx0 = g.nl // g.c
    return pl.pallas_call(
        _attn_kernel,
        grid=(g.b, pairs, nq),
        in_specs=[pl.BlockSpec((tq, pw), lambda b, p, i: (b * nq + i, p)),
                  pl.BlockSpec((g.t, pw), lambda b, p, i: (b, p)),
                  pl.BlockSpec((g.c, pw), lambda b, p, i: (ctx0 + b, p)),
                  pl.BlockSpec((g.t, 2 * MLA_V), lambda b, p, i: (b, p)),
                  pl.BlockSpec((g.c, 2 * MLA_V), lambda b, p, i: (ctx0 + b, p))],
        out_specs=pl.BlockSpec((tq, 2 * MLA_V), lambda b, p, i: (b * nq + i, p)),
        out_shape=jax.ShapeDtypeStruct((g.nl, MLA_HEADS * MLA_V), BF16),
        compiler_params=_cparams("arbitrary", "arbitrary", "arbitrary"),
        name="mla_attention",
    )(q, k, k, v, v)


def _proj_resid_kernel(x_ref, gate_ref, a_ref, w_ref, o_ref):
    o_ref[...] = x_ref[...] + gate_ref[0] * jnp.dot(a_ref[...], w_ref[...], preferred_element_type=F32)


def proj_resid(g, xa, nrows, mods, act, w, tile=512):
    d = xa.shape[1]
    tile = _row_tile(g, tile)
    return pl.pallas_call(
        _proj_resid_kernel,
        grid=(nrows // tile,),
        in_specs=[pl.BlockSpec((tile, d), lambda r: (r, 0)), _mod_spec(g, tile, 2),
                  pl.BlockSpec((tile, act.shape[1]), lambda r: (r, 0)), _full(w.shape)],
        out_specs=pl.BlockSpec((tile, d), lambda r: (r, 0)),
        out_shape=jax.ShapeDtypeStruct((nrows, d), F32),
        compiler_params=_cparams("arbitrary"),
        name="proj_resid",
    )(xa, mods, act, w)


def mla_layer(g, xa, mods, norm_g, w_in, q_norm, q_up, kv_norm, kv_up, w_out):
    scale = (MLA_NOPE + MLA_ROPE) ** -0.5
    half = MLA_ROPE // 2
    o_kv, o_kr = MLA_Q_LORA, MLA_Q_LORA + MLA_KV_LORA
    w_kr = w_in[:, o_kr:]
    padw = lambda w: jnp.pad(w, ((0, 0), (0, LANES - w.shape[1])))
    weights = [w_in[:, :o_kv], w_in[:, o_kv:o_kr], padw(w_kr),
               padw(jnp.concatenate([w_kr[:, half:], w_kr[:, :half]], axis=1))]
    cq, ckv, kr1, kr2 = normmod_mm(g, xa, g.n, norm_g, mods, 0, 1, [w.astype(BF16) for w in weights])

    qu = q_up.reshape(MLA_Q_LORA, MLA_HEADS, MLA_NOPE + MLA_ROPE)
    zq = jnp.zeros((MLA_Q_LORA, MLA_HEADS, HEAD_PAD - MLA_NOPE - MLA_ROPE), F32)
    w1 = jnp.concatenate([qu, zq], axis=2).reshape(MLA_Q_LORA, -1).astype(BF16)
    swapped = jnp.concatenate([qu[:, :, MLA_NOPE + half:], qu[:, :, MLA_NOPE:MLA_NOPE + half]], axis=2)
    w2 = jnp.concatenate([jnp.zeros((MLA_Q_LORA, MLA_HEADS, MLA_NOPE), F32), swapped, zq], axis=2)
    w2 = w2.reshape(MLA_Q_LORA, -1).astype(BF16)
    kvu = kv_up.reshape(MLA_KV_LORA, MLA_HEADS, MLA_NOPE + MLA_V)
    wk = jnp.concatenate([kvu[:, :, :MLA_NOPE], jnp.zeros((MLA_KV_LORA, MLA_HEADS, HEAD_PAD - MLA_NOPE), F32)],
                         axis=2).reshape(MLA_KV_LORA, -1).astype(BF16)
    wv = kvu[:, :, MLA_NOPE:].reshape(MLA_KV_LORA, -1).astype(BF16)
    col = np.arange(MLA_HEADS * HEAD_PAD)
    place = jnp.asarray((col[None, :] % HEAD_PAD) == (np.arange(LANES)[:, None] + MLA_NOPE), BF16)
    place = place * jnp.asarray(np.arange(LANES)[:, None] < MLA_ROPE, BF16)

    q_cos, q_sin, k_cos, k_sin = _rope_tables(g, scale)
    tile = _row_tile(g, 256, within_seq=True)
    q = mla_queries(g, cq, q_norm, w1, w2, q_cos, q_sin, tile)
    k, v = mla_keys_values(g, ckv, kr1, kr2, kv_norm, wk, wv, place, k_cos, k_sin, tile)
    attn = mla_attention(g, q, k, v, math.gcd(256, g.t))
    return proj_resid(g, xa, g.nl, mods, attn, w_out.astype(BF16))


def kernel(x, c, ctx, c_ctx, ada_w, ada_b, norm_mix, norm_ffn, ev_w_in, ev_w_out, rw_mu, rw_w0, rw_w2, rw_a0, rw_a2, rw_g2, rw_kk, rw_ka, rw_rk, rw_gn_w, rw_gn_b, ssm_conv_w, ssm_conv_b, ssm_dt_bias, ssm_a_log, ssm_d, ssm_norm_w, mla_w_in, mla_q_norm, mla_q_up, mla_kv_norm, mla_kv_up, mla_w_out, router_w, router_bias, exp_w1, exp_w3, exp_w2, sh_w1, sh_w3, sh_w2, final_norm):
    bsz, t, d = x.shape
    g = Geom(bsz, t, ctx.shape[1])
    depth = ada_w.shape[0]
    assert depth == 2 and d == D_MODEL, "layer 0 is the recurrent mixer, layer 1 the final attention layer"

    rows = -(-(bsz + 1) // SUBLANES) * SUBLANES
    cond = jnp.concatenate([c, c_ctx[None, :], jnp.zeros((rows - bsz - 1, d), F32)], axis=0)
    mods = ada_mods(cond, ada_w, ada_b).reshape(depth, rows, 1, 6 * d)
    xa = jnp.concatenate([x.reshape(g.nl, d), ctx.reshape(g.nc, d)], axis=0)

    xa = even_layer(g, xa, mods[0], norm_mix[0], ev_w_in[0], ev_w_out[0], rw_mu[0], rw_w0[0], rw_w2[0], rw_a0[0],
                    rw_a2[0], rw_g2[0], rw_kk[0], rw_ka[0], rw_rk[0], rw_gn_w[0], rw_gn_b[0], ssm_conv_w[0],
                    ssm_conv_b[0], ssm_dt_bias[0], ssm_a_log[0], ssm_d[0], ssm_norm_w[0])
    xa = moe_layer(g, xa, g.n, mods[0], norm_ffn[0], router_w[0], router_bias[0], exp_w1[0], exp_w3[0], exp_w2[0],
                   sh_w1[0], sh_w3[0], sh_w2[0], final_norm, False)
    xl = mla_layer(g, xa, mods[1], norm_mix[1], mla_w_in[0], mla_q_norm[0], mla_q_up[0], mla_kv_norm[0],
                   mla_kv_up[0], mla_w_out[0])
    xl = moe_layer(g, xl, g.nl, mods[1], norm_ffn[1], router_w[1], router_bias[1], exp_w1[1], exp_w3[1], exp_w2[1],
                   sh_w1[1], sh_w3[1], sh_w2[1], final_norm, True)
    return xl.reshape(bsz, t, d)
```

```python
import functools
import math

import jax
import jax.numpy as jnp
import numpy as np
from jax import lax
from jax.experimental import pallas as pl
from jax.experimental.pallas import tpu as pltpu

F32 = jnp.float32
BF16 = jnp.bfloat16

D_MODEL = 1024
NORM_EPS = 1e-6
GRID_W = 64

RW_HEADS = 8
RW_HEAD = 64
RW_DIM = 512
RW_COLS = 1920
RW_GN_EPS = 64e-5

SSM_HEADS = 16
SSM_HEAD = 64
SSM_DIM = 1024
SSM_GROUPS = 2
SSM_STATE = 128
SSM_CONV = 5
XBC_DIM = 1536
MIX_DIM = 1536

MLA_HEADS = 16
MLA_Q_LORA = 384
MLA_KV_LORA = 256
MLA_NOPE = 64
MLA_ROPE = 32
MLA_V = 64
ROPE_THETA = 10000.0

N_EXPERTS = 64
TOP_K = 6
N_EXPERT_GROUPS = 8
TOPK_GROUPS = 4
EXPERT_FF = 256
ROUTED_SCALE = 2.5
GATE_LANES = 4 * N_EXPERTS

SUBLANES = 8
LANES = 128
HALO = SUBLANES
VMEM_LIMIT = 56 * 2**20


class Geom:
    def __init__(self, b, t, c):
        self.b, self.t, self.c = b, t, c
        self.nl, self.nc = b * t, b * c
        self.n = self.nl + self.nc


def _row_tile(g, want, within_seq=False):
    tile = math.gcd(want, g.t)
    if within_seq:
        tile = math.gcd(tile, g.c)
    assert g.nc % tile == 0 and tile % SUBLANES == 0
    return tile


def _cparams(*sem):
    return pltpu.CompilerParams(dimension_semantics=sem, vmem_limit_bytes=VMEM_LIMIT)


def _full(shape):
    nd = len(shape)
    return pl.BlockSpec(shape, lambda *_: (0,) * nd)


def _bdot(a, b):
    return jnp.dot(a.astype(BF16), b.astype(BF16), preferred_element_type=F32)


def _split3(a):
    a1 = a.astype(BF16)
    r = a - a1.astype(F32)
    a2 = r.astype(BF16)
    r = r - a2.astype(F32)
    return a1, a2, r.astype(BF16)


def _xdot_l(a, m):
    return sum(jnp.dot(p, m, preferred_element_type=F32) for p in _split3(a))


def _xdot_r(m, a):
    return sum(jnp.dot(m, p, preferred_element_type=F32) for p in _split3(a))


def _sigmoid(x):
    return 0.5 * (1.0 + jnp.tanh(0.5 * x))


def _silu(x):
    return x * _sigmoid(x)


def _softplus(x):
    return jnp.maximum(x, 0.0) + jnp.log(1.0 + jnp.exp(-jnp.abs(x)))


def _seq_pos(r, tile, g):
    n_lat = g.nl // tile
    per_l, per_c = g.t // tile, g.c // tile
    is_lat = r < n_lat
    pos = jnp.where(is_lat, r % per_l, (r - n_lat) % per_c)
    per = jnp.where(is_lat, per_l, per_c)
    return pos == 0, pos == per - 1


def _mod_spec(g, tile, col):
    return pl.BlockSpec((1, 1, D_MODEL), lambda r, *_: (jnp.minimum(r * tile // g.t, g.b), 0, col))


def _halo_specs(g, tile, width):
    nb = g.n // HALO
    per = tile // HALO
    prev = pl.BlockSpec((HALO, width), lambda r: (jnp.maximum(r * per - 1, 0), 0))
    nxt = pl.BlockSpec((HALO, width), lambda r: (jnp.minimum((r + 1) * per, nb - 1), 0))
    return prev, nxt


def _fill_halo(scr, cur_ref, prev_ref, next_ref, first, last, tile):
    scr[pl.ds(0, HALO), :] = prev_ref[...] * jnp.where(first, 0.0, 1.0)
    scr[pl.ds(HALO, tile), :] = cur_ref[...]
    scr[pl.ds(HALO + tile, HALO), :] = next_ref[...] * jnp.where(last, 0.0, 1.0)


def _mods_kernel(c_ref, w_ref, b_ref, o_ref):
    o_ref[0] = _bdot(_silu(c_ref[...]), w_ref[0]) + b_ref[0]


def ada_mods(cond, ada_w, ada_b):
    depth, d, n6 = ada_w.shape
    rows = cond.shape[0]
    tn = 1536
    return pl.pallas_call(
        _mods_kernel,
        grid=(depth, n6 // tn),
        in_specs=[pl.BlockSpec((rows, d), lambda l, j: (0, 0)),
                  pl.BlockSpec((1, d, tn), lambda l, j: (l, 0, j)),
                  pl.BlockSpec((1, 1, tn), lambda l, j: (l, 0, j))],
        out_specs=pl.BlockSpec((1, rows, tn), lambda l, j: (l, 0, j)),
        out_shape=jax.ShapeDtypeStruct((depth, rows, n6), F32),
        compiler_params=_cparams("arbitrary", "arbitrary"),
        name="ada_mods",
    )(cond, ada_w, ada_b.reshape(depth, 1, n6))


def _normmod(x, gain, shift, scale):
    ms = jnp.mean(x * x, axis=-1, keepdims=True)
    h = x * lax.rsqrt(ms + NORM_EPS) * gain
    return h * (1.0 + scale) + shift


def _normmod_mm_kernel(x_ref, g_ref, sh_ref, sc_ref, *refs):
    nw = len(refs) // 2
    h = _normmod(x_ref[...], g_ref[...], sh_ref[0], sc_ref[0]).astype(BF16)
    for w_ref, o_ref in zip(refs[:nw], refs[nw:]):
        o_ref[...] = jnp.dot(h, w_ref[...], preferred_element_type=F32).astype(o_ref.dtype)


def normmod_mm(g, xa, nrows, gain, mods, shift_col, scale_col, weights, tile=256):
    d = xa.shape[1]
    tile = _row_tile(g, tile)
    return pl.pallas_call(
        _normmod_mm_kernel,
        grid=(nrows // tile,),
        in_specs=[pl.BlockSpec((tile, d), lambda r: (r, 0)), _full((1, d)),
                  _mod_spec(g, tile, shift_col), _mod_spec(g, tile, scale_col)]
                 + [_full(w.shape) for w in weights],
        out_specs=[pl.BlockSpec((tile, w.shape[1]), lambda r: (r, 0)) for w in weights],
        out_shape=[jax.ShapeDtypeStruct((nrows, w.shape[1]), F32) for w in weights],
        compiler_params=_cparams("arbitrary"),
        name="normmod_mm",
    )(xa, gain.reshape(1, d), mods, mods, *weights)


def _head_ones():
    i = np.arange(RW_DIM) // RW_HEAD
    return jnp.asarray(i[:, None] == i[None, :], BF16)


def _rw_prep_kernel(g, tile, cur_ref, prev_ref, next_ref, mu_ref, w0_ref, w2_ref, a0_ref, a2_ref, g2_ref,
                    kk_ref, ones_ref, r_out, k_out, v_out, kk_out, w_out, a_out, g_out, scr):
    first, last = _seq_pos(pl.program_id(0), tile, g)
    _fill_halo(scr, cur_ref, prev_ref, next_ref, first, last, tile)
    x = cur_ref[...]
    nb = 0.5 * (scr[pl.ds(HALO - 1, tile), :] + scr[pl.ds(HALO + 1, tile), :])
    mixed = x + mu_ref[...] * (nb - x)
    r = mixed[:, 0:512]
    k = mixed[:, 512:1024]
    v = mixed[:, 1024:1536]
    xw = mixed[:, 1536:1664]
    xa = mixed[:, 1664:1792]
    xg = mixed[:, 1792:1920]
    w_log = -_softplus(-(w0_ref[...] + _bdot(jnp.tanh(xw), w2_ref[...]))) - 0.5
    w_out[...] = jnp.exp(-jnp.exp(w_log))
    a_out[...] = _sigmoid(a0_ref[...] + _bdot(xa, a2_ref[...]))
    g_out[...] = _bdot(_sigmoid(xg), g2_ref[...])
    kkr = k * kk_ref[...]
    ss = _xdot_l(kkr * kkr, ones_ref[...])
    r_out[...] = r
    k_out[...] = k
    v_out[...] = v
    kk_out[...] = kkr * lax.rsqrt(ss + 1e-12)


def _blockdiag2(w):
    z = jnp.zeros_like(w[0])
    return jnp.concatenate([jnp.concatenate([w[0], z], 1), jnp.concatenate([z, w[1]], 1)], 0)


def rw_prep(g, rw, mu, w0, w2, a0, a2, g2, k_k, tile):
    n = g.n
    prev, nxt = _halo_specs(g, tile, RW_COLS)
    row = lambda w: pl.BlockSpec((tile, w), lambda r: (r, 0))
    shp = lambda w: jax.ShapeDtypeStruct((n, w), F32)
    return pl.pallas_call(
        functools.partial(_rw_prep_kernel, g, tile),
        grid=(n // tile,),
        in_specs=[row(RW_COLS), prev, nxt, _full((1, RW_COLS)), _full((1, 1024)), _full((128, 1024)),
                  _full((1, 1024)), _full((128, 1024)), _full((128, 512)), _full((1, 512)), _full((512, 512))],
        out_specs=[row(512), row(512), row(512), row(512), row(1024), row(1024), row(512)],
        out_shape=[shp(512), shp(512), shp(512), shp(512), shp(1024), shp(1024), shp(512)],
        scratch_shapes=[pltpu.VMEM((tile + 2 * HALO, RW_COLS), F32)],
        compiler_params=_cparams("arbitrary"),
        name="rw_prep",
    )(rw, rw, rw, mu.reshape(1, -1), w0.reshape(1, -1), _blockdiag2(w2).astype(BF16), a0.reshape(1, -1),
      _blockdiag2(a2).astype(BF16), g2.astype(BF16), k_k.reshape(1, -1), _head_ones())


def _rw_scan_kernel(tt, r_ref, k_ref, v_ref, kk_ref, w_ref, a_ref, ka_ref, s0_ref, y_ref, sfin_ref,
                    s_scr, b_scr, kd_scr):
    d = pl.program_id(0)

    @pl.when(pl.program_id(2) == 0)
    def _():
        s_scr[...] = s0_ref[0]

    a = a_ref[0]
    b_scr[...] = kk_ref[...] * a
    kd_scr[...] = k_ref[...] * (1.0 + (a - 1.0) * ka_ref[...])
    nv = RW_HEAD // SUBLANES

    def step(i, carry):
        t = i + d * (tt - 1 - 2 * i)
        vt = [v_ref[t, pl.ds(j * SUBLANES, SUBLANES), :] for j in range(nv)]
        sa = [jnp.zeros((SUBLANES, LANES), F32) for _ in range(nv)]
        for k in range(RW_HEAD):
            kkb = kk_ref[t, pl.ds(k, 1), :]
            for j in range(nv):
                sa[j] = sa[j] + s_scr[k, pl.ds(j * SUBLANES, SUBLANES), :] * kkb
        y = [jnp.zeros((SUBLANES, LANES), F32) for _ in range(nv)]
        for k in range(RW_HEAD):
            wb = w_ref[0, t, pl.ds(k, 1), :]
            bb = b_scr[t, pl.ds(k, 1), :]
            kb = kd_scr[t, pl.ds(k, 1), :]
            rb = r_ref[t, pl.ds(k, 1), :]
            for j in range(nv):
                sl = pl.ds(j * SUBLANES, SUBLANES)
                new = s_scr[k, sl, :] * wb + (vt[j] * kb - sa[j] * bb)
                s_scr[k, sl, :] = new
                y[j] = y[j] + new * rb
        for j in range(nv):
            y_ref[0, t, pl.ds(j * SUBLANES, SUBLANES), :] = y[j]
        return carry

    lax.fori_loop(0, tt, step, 0)

    @pl.when(pl.program_id(2) == pl.num_programs(2) - 1)
    def _():
        sfin_ref[0] = s_scr[...]


def rw_scan(r, k, v, kk, w, a, ka, s0, tt):
    s_len, _, chains = r.shape
    nb = s_len // tt

    def tb(d, s):
        return jnp.where(d == 0, s, nb - 1 - s)

    shared = pl.BlockSpec((tt, RW_HEAD, LANES), lambda d, c, s: (tb(d, s), 0, c))
    per_dir = pl.BlockSpec((1, tt, RW_HEAD, LANES), lambda d, c, s: (d, tb(d, s), 0, c))
    state = pl.BlockSpec((1, RW_HEAD, RW_HEAD, LANES), lambda d, c, s: (d, 0, 0, c))
    return pl.pallas_call(
        functools.partial(_rw_scan_kernel, tt),
        grid=(2, chains // LANES, nb),
        in_specs=[shared, shared, shared, shared, per_dir, per_dir,
                  pl.BlockSpec((RW_HEAD, LANES), lambda d, c, s: (0, c)), state],
        out_specs=[per_dir, state],
        out_shape=[jax.ShapeDtypeStruct((2, s_len, RW_HEAD, chains), F32),
                   jax.ShapeDtypeStruct((2, RW_HEAD, RW_HEAD, chains), F32)],
        scratch_shapes=[pltpu.VMEM((RW_HEAD, RW_HEAD, LANES), F32), pltpu.VMEM((tt, RW_HEAD, LANES), F32),
                        pltpu.VMEM((tt, RW_HEAD, LANES), F32)],
        compiler_params=_cparams("arbitrary", "arbitrary", "arbitrary"),
        name="rw_scan",
    )(r, k, v, kk, w, a, ka, s0)


def _to_chains(b, u):
    s = u.shape[0] // b
    return u.reshape(b, s, RW_HEADS, RW_HEAD).transpose(1, 3, 0, 2).reshape(s, RW_HEAD, b * RW_HEADS)


def _to_chains2(b, u):
    s = u.shape[0] // b
    return u.reshape(b, s, 2, RW_HEADS, RW_HEAD).transpose(2, 1, 4, 0, 3).reshape(2, s, RW_HEAD, b * RW_HEADS)


def _from_chains2(b, y):
    s = y.shape[1]
    return y.reshape(2, s, RW_HEAD, b, RW_HEADS).transpose(3, 1, 0, 4, 2).reshape(b * s, 2 * RW_DIM)


def _conv_kernel(g, tile, cur_ref, prev_ref, next_ref, w_ref, b_ref, o_ref, scr):
    first, last = _seq_pos(pl.program_id(0), tile, g)
    _fill_halo(scr, cur_ref, prev_ref, next_ref, first, last, tile)
    acc = b_ref[...] + w_ref[pl.ds(0, 1), :] * scr[pl.ds(HALO - 2, tile), :]
    for j in range(1, SSM_CONV):
        acc = acc + w_ref[pl.ds(j, 1), :] * scr[pl.ds(HALO - 2 + j, tile), :]
    o_ref[...] = _silu(acc)


def ssm_conv(g, xbc, conv_w, conv_b, tile):
    prev, nxt = _halo_specs(g, tile, XBC_DIM)
    row = pl.BlockSpec((tile, XBC_DIM), lambda r: (r, 0))
    return pl.pallas_call(
        functools.partial(_conv_kernel, g, tile),
        grid=(g.n // tile,),
        in_specs=[row, prev, nxt, _full((SSM_CONV, XBC_DIM)), _full((1, XBC_DIM))],
        out_specs=row,
        out_shape=jax.ShapeDtypeStruct((g.n, XBC_DIM), F32),
        scratch_shapes=[pltpu.VMEM((tile + 2 * HALO, XBC_DIM), F32)],
        compiler_params=_cparams("arbitrary"),
        name="ssm_conv",
    )(xbc, xbc, xbc, conv_w, conv_b.reshape(1, -1))


def _ssd_dir(rev, xbc, dtr, bias, aneg, expand, tri, st_ref):
    cl = xbc.shape[0]
    xs = xbc[:, 0:SSM_DIM]
    dt = _softplus(dtr + bias)
    da = dt * aneg
    cs = _xdot_r(tri, da)
    dt_x = _xdot_l(dt, expand)
    cs_x = _xdot_l(cs, expand)
    cs_end = cs_x[0:1, :] if rev else cs_x[cl - 1:cl, :]
    xdt = xs * dt_x
    cs_t = cs.T
    row = lax.broadcasted_iota(jnp.int32, (cl, cl), 0)
    col = lax.broadcasted_iota(jnp.int32, (cl, cl), 1)
    keep = (col >= row) if rev else (col <= row)
    lane = lax.broadcasted_iota(jnp.int32, (cl, LANES), 1)
    per_group = SSM_DIM // SSM_GROUPS
    y_parts = []
    for gi in range(SSM_GROUPS):
        bm = xbc[:, SSM_DIM + gi * SSM_STATE:SSM_DIM + (gi + 1) * SSM_STATE]
        cm = xbc[:, SSM_DIM + SSM_GROUPS * SSM_STATE + gi * SSM_STATE:
                 SSM_DIM + SSM_GROUPS * SSM_STATE + (gi + 1) * SSM_STATE]
        cb = lax.dot_general(cm.astype(BF16), bm.astype(BF16), (((1,), (1,)), ((), ())),
                             preferred_element_type=F32)
        gs = slice(gi * per_group, (gi + 1) * per_group)
        s_in = st_ref[:, gs]
        y_off = _bdot(cm, s_in) * jnp.exp(cs_x[:, gs])
        for p in range(per_group // LANES):
            h0 = (gi * per_group + p * LANES) // SSM_HEAD
            x_pair = xdt[:, h0 * SSM_HEAD:h0 * SSM_HEAD + LANES].astype(BF16)
            ys = []
            for hh in (h0, h0 + 1):
                seg = cs[:, hh:hh + 1] - cs_t[hh:hh + 1, :]
                m = cb * jnp.exp(jnp.where(keep, seg, -jnp.inf))
                ys.append(jnp.dot(m.astype(BF16), x_pair, preferred_element_type=F32))
            y_parts.append(jnp.where(lane < SSM_HEAD, ys[0], ys[1]) + y_off[:, p * LANES:(p + 1) * LANES])
        xd = (xdt[:, gs] * jnp.exp(cs_end[:, gs] - cs_x[:, gs])).astype(BF16)
        st_new = lax.dot_general(bm.astype(BF16), xd, (((0,), (0,)), ((), ())), preferred_element_type=F32)
        st_ref[:, gs] = s_in * jnp.exp(cs_end[:, gs]) + st_new
    return jnp.concatenate(y_parts, axis=1)


def _ssd_kernel(xf_ref, dtf_ref, xb_ref, dtb_ref, bias_ref, aneg_ref, exp_ref, tril_ref, triu_ref,
                yf_ref, yb_ref, st_scr):
    @pl.when(pl.program_id(1) == 0)
    def _():
        st_scr[...] = jnp.zeros_like(st_scr)

    yf_ref[...] = _ssd_dir(False, xf_ref[...], dtf_ref[...], bias_ref[0], aneg_ref[0], exp_ref[...],
                           tril_ref[...], st_scr.at[0])
    yb_ref[...] = _ssd_dir(True, xb_ref[...], dtb_ref[...], bias_ref[1], aneg_ref[1], exp_ref[...],
                           triu_ref[...], st_scr.at[1])


def ssd_scan(g, xbc_act, dt_f, dt_b, dt_bias, a_log, cl):
    ncc, nlc = g.c // cl, g.t // cl
    ns = ncc + nlc

    def rows(b, c):
        return jnp.where(c < ncc, g.nl // cl + b * ncc + c, b * nlc + (c - ncc))

    def fwd(b, s):
        return rows(b, s)

    def bwd(b, s):
        return rows(b, jnp.where(s < ncc, ncc - 1 - s, ns - 1 - (s - ncc)))

    pad = lambda u: jnp.pad(u, ((0, 0), (0, LANES - SSM_HEADS))).reshape(2, 1, LANES)
    expand = jnp.asarray(np.arange(LANES)[:, None] == (np.arange(SSM_DIM) // SSM_HEAD)[None, :], BF16)
    idx = np.arange(cl)
    tril = jnp.asarray(idx[:, None] >= idx[None, :], BF16)
    triu = jnp.asarray(idx[:, None] <= idx[None, :], BF16)
    spec = lambda w, f: pl.BlockSpec((cl, w), lambda b, s: (f(b, s), 0))
    return pl.pallas_call(
        _ssd_kernel,
        grid=(g.b, ns),
        in_specs=[spec(XBC_DIM, fwd), spec(LANES, fwd), spec(XBC_DIM, bwd), spec(LANES, bwd),
                  _full((2, 1, LANES)), _full((2, 1, LANES)), _full((LANES, SSM_DIM)), _full((cl, cl)),
                  _full((cl, cl))],
        out_specs=[spec(SSM_DIM, fwd), spec(SSM_DIM, bwd)],
        out_shape=[jax.ShapeDtypeStruct((g.n, SSM_DIM), F32)] * 2,
        scratch_shapes=[pltpu.VMEM((2, SSM_STATE, SSM_DIM), F32)],
        compiler_params=_cparams("arbitrary", "arbitrary"),
        name="ssd_scan",
    )(xbc_act, dt_f, xbc_act, dt_b, pad(dt_bias), pad(-jnp.exp(a_log)), expand, tril, triu)


def _even_out_kernel(n_lat, x_ref, gate_ref, yl_ref, yc_ref, r_ref, k_ref, v_ref, a_ref, g_ref,
                     ysf_ref, ysb_ref, xs_ref, z_ref, ka_ref, rk_ref, gnw_ref, gnb_ref, ones_ref,
                     dsk_ref, nw_ref, wo_rw_ref, wo_ss_ref, o_ref):
    ones = ones_ref[...]
    y2 = jnp.where(pl.program_id(0) < n_lat, yl_ref[...], yc_ref[...])
    y = y2[:, 0:RW_DIM] + y2[:, RW_DIM:]
    mean = _xdot_l(y, ones) * (1.0 / RW_HEAD)
    dev = y - mean
    var = _xdot_l(dev * dev, ones) * (1.0 / RW_HEAD)
    yn = dev * lax.rsqrt(var + RW_GN_EPS) * gnw_ref[...] + gnb_ref[...]
    r, k, a, ka = r_ref[...], k_ref[...], a_ref[...], ka_ref[...]
    ksum = k * (1.0 + (a[:, 0:RW_DIM] - 1.0) * ka) + k * (1.0 + (a[:, RW_DIM:] - 1.0) * ka)
    bonus = _xdot_l(r * ksum * rk_ref[...], ones) * v_ref[...]
    out_rw = (yn + bonus) * g_ref[...]

    ys = ysf_ref[...] + ysb_ref[...] + xs_ref[...] * dsk_ref[...]
    ys = ys * _silu(z_ref[...])
    half = SSM_DIM // SSM_GROUPS
    parts = []
    for gi in range(SSM_GROUPS):
        yg = ys[:, gi * half:(gi + 1) * half]
        parts.append(yg * lax.rsqrt(jnp.mean(yg * yg, axis=-1, keepdims=True) + NORM_EPS))
    out_ss = jnp.concatenate(parts, axis=1) * nw_ref[...]
    o = _bdot(out_rw, wo_rw_ref[...]) + _bdot(out_ss, wo_ss_ref[...])
    o_ref[...] = x_ref[...] + gate_ref[0] * o


def even_out(g, xa, mods, y_lat, y_ctx, r, k, v, a, gg, ys_f, ys_b, xbc_act, z, ka, rk, gn_w, gn_b,
             d_skip, norm_w, w_out, tile=256):
    tile = _row_tile(g, tile)
    n_lat = g.nl // tile
    row = lambda w: pl.BlockSpec((tile, w), lambda i: (i, 0))
    vec = lambda w: _full((1, w))
    lat = pl.BlockSpec((tile, 2 * RW_DIM), lambda i: (jnp.minimum(i, n_lat - 1), 0))
    ctx = pl.BlockSpec((tile, 2 * RW_DIM), lambda i: (jnp.maximum(i - n_lat, 0), 0))
    return pl.pallas_call(
        functools.partial(_even_out_kernel, n_lat),
        grid=(g.n // tile,),
        in_specs=[row(D_MODEL), _mod_spec(g, tile, 2), lat, ctx, row(512), row(512), row(512),
                  row(1024), row(512), row(1024), row(1024), row(1024), row(1024),
                  vec(512), vec(512), vec(512), vec(512), _full((512, 512)), vec(1024), vec(1024),
                  _full((RW_DIM, D_MODEL)), _full((SSM_DIM, D_MODEL))],
        out_specs=row(D_MODEL),
        out_shape=jax.ShapeDtypeStruct((g.n, D_MODEL), F32),
        compiler_params=_cparams("arbitrary"),
        name="even_out",
    )(xa, mods, y_lat, y_ctx, r, k, v, a, gg, ys_f, ys_b, xbc_act, z, ka.reshape(1, -1), rk.reshape(1, -1),
      gn_w.reshape(1, -1), gn_b.reshape(1, -1), _head_ones(), jnp.repeat(d_skip, SSM_HEAD).reshape(1, -1),
      norm_w.reshape(1, -1), w_out[:RW_DIM].astype(BF16), w_out[RW_DIM:].astype(BF16))


def even_layer(g, xa, mods, norm_g, w_in, w_out, mu, w0, w2, a0, a2, g2, k_k, k_a, r_k, gn_w, gn_b,
               conv_w, conv_b, dt_bias, a_log, d_skip, ssm_norm_w):
    o_z, o_x, o_dt = RW_COLS, RW_COLS + SSM_DIM, RW_COLS + SSM_DIM + XBC_DIM
    padw = lambda w: jnp.pad(w, ((0, 0), (0, LANES - w.shape[1])))
    weights = [w_in[:, :o_z], w_in[:, o_z:o_x], w_in[:, o_x:o_dt], padw(w_in[:, o_dt:o_dt + SSM_HEADS]),
               padw(w_in[:, o_dt + SSM_HEADS:])]
    rw, z, xbc, dt_f, dt_b = normmod_mm(g, xa, g.n, norm_g, mods, 0, 1, [w.astype(BF16) for w in weights])

    seq_tile = _row_tile(g, 256, within_seq=True)
    r, k, v, kk, w, a, gg = rw_prep(g, rw, mu, w0, w2, a0, a2, g2, k_k, seq_tile)
    ka_c = jnp.tile(k_a.reshape(RW_HEADS, RW_HEAD).T, (1, g.b))
    state = jnp.zeros((2, RW_HEAD, RW_HEAD, g.b * RW_HEADS), F32)
    ys = []
    for sl in (slice(g.nl, g.n), slice(0, g.nl)):
        one = lambda u: _to_chains(g.b, u[sl])
        two = lambda u: _to_chains2(g.b, u[sl])
        y, state = rw_scan(one(r), one(k), one(v), one(kk), two(w), two(a), ka_c, state, math.gcd(32, g.c))
        ys.append(_from_chains2(g.b, y))

    xbc_act = ssm_conv(g, xbc, conv_w, conv_b, seq_tile)
    ys_f, ys_b = ssd_scan(g, xbc_act, dt_f, dt_b, dt_bias, a_log, math.gcd(128, g.c))
    return even_out(g, xa, mods, ys[1], ys[0], r, k, v, a, gg, ys_f, ys_b, xbc_act, z, k_a, r_k, gn_w, gn_b,
                    d_skip, ssm_norm_w, w_out)


def _router_kernel(x_ref, g_ref, sh_ref, sc_ref, rw_ref, rb_ref, h_ref, gates_ref):
    h = _normmod(x_ref[...], g_ref[...], sh_ref[0], sc_ref[0])
    h_ref[...] = h.astype(BF16)
    nt = (((1,), (1,)), ((), ()))
    h3, w3 = _split3(h), _split3(rw_ref[...])
    logits = sum(lax.dot_general(w3[i], h3[j], nt, preferred_element_type=F32)
                 for i in range(3) for j in range(3 - i))
    scores = _sigmoid(logits)
    sel = scores + rb_ref[...]
    tile = sel.shape[1]
    per = N_EXPERTS // N_EXPERT_GROUPS
    grp = sel.reshape(N_EXPERT_GROUPS, per, tile)
    pos = lax.broadcasted_iota(jnp.int32, grp.shape, 1).astype(F32)
    m1 = jnp.max(grp, axis=1, keepdims=True)
    first = jnp.min(jnp.where(grp == m1, pos, float(per)), axis=1, keepdims=True)
    m2 = jnp.max(jnp.where(pos == first, -jnp.inf, grp), axis=1, keepdims=True)
    gsc = jnp.broadcast_to(m1 + m2, grp.shape)
    gid = lax.broadcasted_iota(jnp.int32, grp.shape, 0)
    beaten = jnp.zeros(grp.shape, F32)
    for o in range(N_EXPERT_GROUPS):
        other = gsc[o:o + 1]
        beaten = beaten + jnp.where((other > gsc) | ((other == gsc) & (o < gid)), 1.0, 0.0)
    cand = jnp.where(beaten < TOPK_GROUPS, grp, -jnp.inf).reshape(N_EXPERTS, tile)
    eid = lax.broadcasted_iota(jnp.int32, cand.shape, 0).astype(F32)
    chosen = jnp.zeros(cand.shape, F32)
    for _ in range(TOP_K):
        best = jnp.max(cand, axis=0, keepdims=True)
        pick = eid == jnp.min(jnp.where(cand == best, eid, float(N_EXPERTS)), axis=0, keepdims=True)
        chosen = jnp.where(pick, 1.0, chosen)
        cand = jnp.where(pick, -jnp.inf, cand)
    wts = scores * chosen
    gates = wts / jnp.sum(wts, axis=0, keepdims=True) * ROUTED_SCALE
    terms = [t.astype(F32) for t in _split3(gates)] + [jnp.zeros_like(gates)]
    gates_ref[...] = jnp.concatenate(terms, axis=0).T.astype(BF16)


def moe_router(g, xa, nrows, gain, mods, router_w, router_bias, tile=512):
    d = xa.shape[1]
    tile = _row_tile(g, tile)
    return pl.pallas_call(
        _router_kernel,
        grid=(nrows // tile,),
        in_specs=[pl.BlockSpec((tile, d), lambda r: (r, 0)), _full((1, d)), _mod_spec(g, tile, 3),
                  _mod_spec(g, tile, 4), _full((N_EXPERTS, d)), _full((N_EXPERTS, 1))],
        out_specs=[pl.BlockSpec((tile, d), lambda r: (r, 0)), pl.BlockSpec((tile, GATE_LANES), lambda r: (r, 0))],
        out_shape=[jax.ShapeDtypeStruct((nrows, d), BF16), jax.ShapeDtypeStruct((nrows, GATE_LANES), BF16)],
        compiler_params=_cparams("arbitrary"),
        name="moe_router",
    )(xa, gain.reshape(1, d), mods, mods, router_w.T, router_bias.reshape(N_EXPERTS, 1))


EXPERTS_PER_STEP = 4


def _experts_kernel(final, h_ref, gates_ref, w1_ref, w3_ref, w2_ref, sw1_ref, sw3_ref, sw2_ref, x_ref, gate_ref,
                    fin_ref, o_ref, acc):
    e = pl.program_id(1)
    h = h_ref[...]
    ff = EXPERT_FF

    def ffn(w1, w3):
        return _silu(jnp.dot(h, w1, preferred_element_type=F32)) * jnp.dot(h, w3, preferred_element_type=F32)

    @pl.when(e == 0)
    def _():
        acc[...] = jnp.dot(ffn(sw1_ref[...], sw3_ref[...]).astype(BF16), sw2_ref[...],
                           preferred_element_type=F32)

    width = EXPERTS_PER_STEP * ff
    row = lax.broadcasted_iota(jnp.int32, (GATE_LANES, width), 0)
    col = lax.broadcasted_iota(jnp.int32, (GATE_LANES, width), 1)
    hot = ((jnp.bitwise_and(row, N_EXPERTS - 1) == e * EXPERTS_PER_STEP + jnp.right_shift(col, int(math.log2(ff))))
           & (row < 3 * N_EXPERTS))
    gcols = jnp.dot(gates_ref[...], hot.astype(BF16), preferred_element_type=F32)
    hid = jnp.concatenate([ffn(w1_ref[j], w3_ref[j]) for j in range(EXPERTS_PER_STEP)], axis=1) * gcols
    acc[...] += jnp.dot(hid.astype(BF16), w2_ref[0], preferred_element_type=F32)

    @pl.when(e == pl.num_programs(1) - 1)
    def _():
        out = x_ref[...] + gate_ref[0] * acc[...]
        if final:
            ms = jnp.mean(out * out, axis=-1, keepdims=True)
            out = out * lax.rsqrt(ms + NORM_EPS) * fin_ref[...]
        o_ref[...] = out


def moe_experts(g, xa, nrows, mods, h, gates, w1, w3, w2, sw1, sw3, sw2, final_gain, final, tile=1024):
    d = xa.shape[1]
    tile = _row_tile(g, tile)
    eps = EXPERTS_PER_STEP
    w2c = w2.astype(BF16).reshape(N_EXPERTS // eps, eps * EXPERT_FF, d)
    up = pl.BlockSpec((eps, d, EXPERT_FF), lambda i, e: (e, 0, 0))
    return pl.pallas_call(
        functools.partial(_experts_kernel, final),
        grid=(nrows // tile, N_EXPERTS // eps),
        in_specs=[pl.BlockSpec((tile, d), lambda i, e: (i, 0)),
                  pl.BlockSpec((tile, GATE_LANES), lambda i, e: (i, 0)),
                  up, up, pl.BlockSpec((1, eps * EXPERT_FF, d), lambda i, e: (e, 0, 0)),
                  _full(sw1.shape), _full(sw3.shape), _full(sw2.shape),
                  pl.BlockSpec((tile, d), lambda i, e: (i, 0)), _mod_spec(g, tile, 5), _full((1, d))],
        out_specs=pl.BlockSpec((tile, d), lambda i, e: (i, 0)),
        out_shape=jax.ShapeDtypeStruct((nrows, d), F32),
        scratch_shapes=[pltpu.VMEM((tile, d), F32)],
        compiler_params=_cparams("arbitrary", "arbitrary"),
        name="moe_experts",
    )(h, gates, w1.astype(BF16), w3.astype(BF16), w2c, sw1.astype(BF16), sw3.astype(BF16), sw2.astype(BF16),
      xa, mods, final_gain.reshape(1, d))


def moe_layer(g, xa, nrows, mods, gain, router_w, router_bias, w1, w3, w2, sw1, sw3, sw2, final_gain, final):
    h, gates = moe_router(g, xa, nrows, gain, mods, router_w, router_bias)
    return moe_experts(g, xa, nrows, mods, h, gates, w1, w3, w2, sw1, sw3, sw2, final_gain, final)


HEAD_PAD = 128


def _rope_tables(g, scale):
    rows = g.t // GRID_W
    r_idx = jnp.repeat(jnp.arange(rows, dtype=F32), GRID_W)
    c_idx = jnp.tile(jnp.arange(GRID_W, dtype=F32), rows)
    axis_dim = MLA_ROPE // 2
    inv_freq = ROPE_THETA ** (-jnp.arange(0, axis_dim, 2, dtype=F32) / axis_dim)
    ang = jnp.concatenate([r_idx[:, None] * inv_freq, c_idx[:, None] * inv_freq], axis=-1)
    cos, sin = jnp.cos(ang), jnp.sin(ang)
    cos2 = jnp.concatenate([cos, cos], axis=1)
    sin2 = jnp.concatenate([-sin, sin], axis=1)
    t = g.t
    q_cos = jnp.concatenate([jnp.ones((t, MLA_NOPE), F32), cos2, jnp.zeros((t, 32), F32)], axis=1) * scale
    q_sin = jnp.concatenate([jnp.zeros((t, MLA_NOPE), F32), sin2, jnp.zeros((t, 32), F32)], axis=1) * scale
    zpad = jnp.zeros((t, LANES - MLA_ROPE), F32)
    k_cos = jnp.concatenate([cos2, zpad], axis=1)
    k_sin = jnp.concatenate([sin2, zpad], axis=1)
    c_cos = jnp.concatenate([jnp.ones((g.c, MLA_ROPE), F32), jnp.zeros((g.c, LANES - MLA_ROPE), F32)], axis=1)
    k_cos = jnp.concatenate([k_cos, c_cos], axis=0)
    k_sin = jnp.concatenate([k_sin, jnp.zeros((g.c, LANES), F32)], axis=0)
    return q_cos, q_sin, k_cos, k_sin


def _q_kernel(cq_ref, qn_ref, w1_ref, w2_ref, cos_ref, sin_ref, q_ref):
    cq = cq_ref[...]
    qn = (cq * lax.rsqrt(jnp.mean(cq * cq, axis=-1, keepdims=True) + NORM_EPS) * qn_ref[...]).astype(BF16)
    q1 = jnp.dot(qn, w1_ref[...], preferred_element_type=F32)
    q2 = jnp.dot(qn, w2_ref[...], preferred_element_type=F32)
    cos, sin = cos_ref[...], sin_ref[...]
    for h in range(MLA_HEADS):
        sl = slice(h * HEAD_PAD, (h + 1) * HEAD_PAD)
        q_ref[:, sl] = (q1[:, sl] * cos + q2[:, sl] * sin).astype(BF16)


def mla_queries(g, cq, q_norm, w1, w2, q_cos, q_sin, tile):
    per = g.t // tile
    width = MLA_HEADS * HEAD_PAD
    return pl.pallas_call(
        _q_kernel,
        grid=(g.nl // tile,),
        in_specs=[pl.BlockSpec((tile, MLA_Q_LORA), lambda r: (r, 0)), _full((1, MLA_Q_LORA)),
                  _full(w1.shape), _full(w2.shape),
                  pl.BlockSpec((tile, HEAD_PAD), lambda r: (r % per, 0)),
                  pl.BlockSpec((tile, HEAD_PAD), lambda r: (r % per, 0))],
        out_specs=pl.BlockSpec((tile, width), lambda r: (r, 0)),
        out_shape=jax.ShapeDtypeStruct((g.nl, width), BF16),
        compiler_params=_cparams("arbitrary"),
        name="mla_queries",
    )(cq, q_norm.reshape(1, -1), w1, w2, q_cos, q_sin)


def _kv_kernel(ckv_ref, kr1_ref, kr2_ref, kvn_ref, wk_ref, wv_ref, place_ref, cos_ref, sin_ref, k_ref, v_ref):
    ckv = ckv_ref[...]
    kvn = (ckv * lax.rsqrt(jnp.mean(ckv * ckv, axis=-1, keepdims=True) + NORM_EPS) * kvn_ref[...]).astype(BF16)
    kr = (kr1_ref[...] * cos_ref[...] + kr2_ref[...] * sin_ref[...]).astype(BF16)
    k = jnp.dot(kvn, wk_ref[...], preferred_element_type=F32) + jnp.dot(kr, place_ref[...], preferred_element_type=F32)
    k_ref[...] = k.astype(BF16)
    v_ref[...] = jnp.dot(kvn, wv_ref[...], preferred_element_type=F32).astype(BF16)


def mla_keys_values(g, ckv, kr1, kr2, kv_norm, wk, wv, place, k_cos, k_sin, tile):
    n_lat, per_l, per_c = g.nl // tile, g.t // tile, g.c // tile

    def tab(r):
        return (jnp.where(r < n_lat, r % per_l, per_l + (r - n_lat) % per_c), 0)

    row = lambda w: pl.BlockSpec((tile, w), lambda r: (r, 0))
    kw = MLA_HEADS * HEAD_PAD
    return pl.pallas_call(
        _kv_kernel,
        grid=(g.n // tile,),
        in_specs=[row(MLA_KV_LORA), row(LANES), row(LANES), _full((1, MLA_KV_LORA)), _full(wk.shape),
                  _full(wv.shape), _full(place.shape), pl.BlockSpec((tile, LANES), tab),
                  pl.BlockSpec((tile, LANES), tab)],
        out_specs=[row(kw), row(MLA_HEADS * MLA_V)],
        out_shape=[jax.ShapeDtypeStruct((g.n, kw), BF16), jax.ShapeDtypeStruct((g.n, MLA_HEADS * MLA_V), BF16)],
        compiler_params=_cparams("arbitrary"),
        name="mla_keys_values",
    )(ckv, kr1, kr2, kv_norm.reshape(1, -1), wk, wv, place, k_cos, k_sin)


def _attn_kernel(q_ref, kl_ref, kc_ref, vl_ref, vc_ref, o_ref):
    nt = (((1,), (1,)), ((), ()))
    vl, vc = vl_ref[...], vc_ref[...]
    outs = []
    for hh in range(2):
        sl = slice(hh * HEAD_PAD, (hh + 1) * HEAD_PAD)
        q = q_ref[:, sl]
        s_l = lax.dot_general(q, kl_ref[:, sl], nt, preferred_element_type=F32)
        s_c = lax.dot_general(q, kc_ref[:, sl], nt, preferred_element_type=F32)
        m = jnp.maximum(jnp.max(s_l, axis=-1, keepdims=True), jnp.max(s_c, axis=-1, keepdims=True))
        p_l, p_c = jnp.exp(s_l - m), jnp.exp(s_c - m)
        den = jnp.sum(p_l, axis=-1, keepdims=True) + jnp.sum(p_c, axis=-1, keepdims=True)
        o = (jnp.dot(p_l.astype(BF16), vl, preferred_element_type=F32)
             + jnp.dot(p_c.astype(BF16), vc, preferred_element_type=F32))
        outs.append(o / den)
    lane = lax.broadcasted_iota(jnp.int32, outs[0].shape, 1)
    o_ref[...] = jnp.where(lane < MLA_V, outs[0], outs[1]).astype(o_ref.dtype)


def mla_attention(g, q, k, v, tq):
    nq = g.t // tq
    pairs = MLA_HEADS // 2
    pw = 2 * HEAD_PAD
    ctx0 = g.nl // g.c
    return pl.pallas_call(
        _attn_kernel,
        grid=(g.b, pairs, nq),
        in_specs=[pl.BlockSpec((tq, pw), lambda b, p, i: (b * nq + i, p)),
                  pl.BlockSpec((g.t, pw), lambda b, p, i: (b, p)),
                  pl.BlockSpec((g.c, pw), lambda b, p, i: (ctx0 + b, p)),
                  pl.BlockSpec((g.t, 2 * MLA_V), lambda b, p, i: (b, p)),
                  pl.BlockSpec((g.c, 2 * MLA_V), lambda b, p, i: (ctx0 + b, p))],
        out_specs=pl.BlockSpec((tq, 2 * MLA_V), lambda b, p, i: (b * nq + i, p)),
        out_shape=jax.ShapeDtypeStruct((g.nl, MLA_HEADS * MLA_V), BF16),
        compiler_params=_cparams("arbitrary", "arbitrary", "arbitrary"),
        name="mla_attention",
    )(q, k, k, v, v)


def _proj_resid_kernel(x_ref, gate_ref, a_ref, w_ref, o_ref):
    o_ref[...] = x_ref[...] + gate_ref[0] * jnp.dot(a_ref[...], w_ref[...], preferred_element_type=F32)


def proj_resid(g, xa, nrows, mods, act, w, tile=512):
    d = xa.shape[1]
    tile = _row_tile(g, tile)
    return pl.pallas_call(
        _proj_resid_kernel,
        grid=(nrows // tile,),
        in_specs=[pl.BlockSpec((tile, d), lambda r: (r, 0)), _mod_spec(g, tile, 2),
                  pl.BlockSpec((tile, act.shape[1]), lambda r: (r, 0)), _full(w.shape)],
        out_specs=pl.BlockSpec((tile, d), lambda r: (r, 0)),
        out_shape=jax.ShapeDtypeStruct((nrows, d), F32),
        compiler_params=_cparams("arbitrary"),
        name="proj_resid",
    )(xa, mods, act, w)


def mla_layer(g, xa, mods, norm_g, w_in, q_norm, q_up, kv_norm, kv_up, w_out):
    scale = (MLA_NOPE + MLA_ROPE) ** -0.5
    half = MLA_ROPE // 2
    o_kv, o_kr = MLA_Q_LORA, MLA_Q_LORA + MLA_KV_LORA
    w_kr = w_in[:, o_kr:]
    padw = lambda w: jnp.pad(w, ((0, 0), (0, LANES - w.shape[1])))
    weights = [w_in[:, :o_kv], w_in[:, o_kv:o_kr], padw(w_kr),
               padw(jnp.concatenate([w_kr[:, half:], w_kr[:, :half]], axis=1))]
    cq, ckv, kr1, kr2 = normmod_mm(g, xa, g.n, norm_g, mods, 0, 1, [w.astype(BF16) for w in weights])

    qu = q_up.reshape(MLA_Q_LORA, MLA_HEADS, MLA_NOPE + MLA_ROPE)
    zq = jnp.zeros((MLA_Q_LORA, MLA_HEADS, HEAD_PAD - MLA_NOPE - MLA_ROPE), F32)
    w1 = jnp.concatenate([qu, zq], axis=2).reshape(MLA_Q_LORA, -1).astype(BF16)
    swapped = jnp.concatenate([qu[:, :, MLA_NOPE + half:], qu[:, :, MLA_NOPE:MLA_NOPE + half]], axis=2)
    w2 = jnp.concatenate([jnp.zeros((MLA_Q_LORA, MLA_HEADS, MLA_NOPE), F32), swapped, zq], axis=2)
    w2 = w2.reshape(MLA_Q_LORA, -1).astype(BF16)
    kvu = kv_up.reshape(MLA_KV_LORA, MLA_HEADS, MLA_NOPE + MLA_V)
    wk = jnp.concatenate([kvu[:, :, :MLA_NOPE], jnp.zeros((MLA_KV_LORA, MLA_HEADS, HEAD_PAD - MLA_NOPE), F32)],
                         axis=2).reshape(MLA_KV_LORA, -1).astype(BF16)
    wv = kvu[:, :, MLA_NOPE:].reshape(MLA_KV_LORA, -1).astype(BF16)
    col = np.arange(MLA_HEADS * HEAD_PAD)
    place = jnp.asarray((col[None, :] % HEAD_PAD) == (np.arange(LANES)[:, None] + MLA_NOPE), BF16)
    place = place * jnp.asarray(np.arange(LANES)[:, None] < MLA_ROPE, BF16)

    q_cos, q_sin, k_cos, k_sin = _rope_tables(g, scale)
    tile = _row_tile(g, 256, within_seq=True)
    q = mla_queries(g, cq, q_norm, w1, w2, q_cos, q_sin, tile)
    k, v = mla_keys_values(g, ckv, kr1, kr2, kv_norm, wk, wv, place, k_cos, k_sin, tile)
    attn = mla_attention(g, q, k, v, math.gcd(256, g.t))
    return proj_resid(g, xa, g.nl, mods, attn, w_out.astype(BF16))


def kernel(x, c, ctx, c_ctx, ada_w, ada_b, norm_mix, norm_ffn, ev_w_in, ev_w_out, rw_mu, rw_w0, rw_w2, rw_a0, rw_a2, rw_g2, rw_kk, rw_ka, rw_rk, rw_gn_w, rw_gn_b, ssm_conv_w, ssm_conv_b, ssm_dt_bias, ssm_a_log, ssm_d, ssm_norm_w, mla_w_in, mla_q_norm, mla_q_up, mla_kv_norm, mla_kv_up, mla_w_out, router_w, router_bias, exp_w1, exp_w3, exp_w2, sh_w1, sh_w3, sh_w2, final_norm):
    bsz, t, d = x.shape
    g = Geom(bsz, t, ctx.shape[1])
    depth = ada_w.shape[0]
    assert depth == 2 and d == D_MODEL, "layer 0 is the recurrent mixer, layer 1 the final attention layer"

    rows = -(-(bsz + 1) // SUBLANES) * SUBLANES
    cond = jnp.concatenate([c, c_ctx[None, :], jnp.zeros((rows - bsz - 1, d), F32)], axis=0)
    mods = ada_mods(cond, ada_w, ada_b).reshape(depth, rows, 1, 6 * d)
    xa = jnp.concatenate([x.reshape(g.nl, d), ctx.reshape(g.nc, d)], axis=0)

    xa = even_layer(g, xa, mods[0], norm_mix[0], ev_w_in[0], ev_w_out[0], rw_mu[0], rw_w0[0], rw_w2[0], rw_a0[0],
                    rw_a2[0], rw_g2[0], rw_kk[0], rw_ka[0], rw_rk[0], rw_gn_w[0], rw_gn_b[0], ssm_conv_w[0],
                    ssm_conv_b[0], ssm_dt_bias[0], ssm_a_log[0], ssm_d[0], ssm_norm_w[0])
    xa = moe_layer(g, xa, g.n, mods[0], norm_ffn[0], router_w[0], router_bias[0], exp_w1[0], exp_w3[0], exp_w2[0],
                   sh_w1[0], sh_w3[0], sh_w2[0], final_norm, False)
    xl = mla_layer(g, xa, mods[1], norm_mix[1], mla_w_in[0], mla_q_norm[0], mla_q_up[0], mla_kv_norm[0],
                   mla_kv_up[0], mla_w_out[0])
    xl = moe_layer(g, xl, g.nl, mods[1], norm_ffn[1], router_w[1], router_bias[1], exp_w1[1], exp_w3[1], exp_w2[1],
                   sh_w1[1], sh_w3[1], sh_w2[1], final_norm, True)
    return xl.reshape(bsz, t, d)
```

```python
import functools
import math

import jax
import jax.numpy as jnp
import numpy as np
from jax import lax
from jax.experimental import pallas as pl
from jax.experimental.pallas import tpu as pltpu

F32 = jnp.float32
BF16 = jnp.bfloat16

D_MODEL = 1024
NORM_EPS = 1e-6
GRID_W = 64

RW_HEADS = 8
RW_HEAD = 64
RW_DIM = 512
RW_COLS = 1920
RW_GN_EPS = 64e-5

SSM_HEADS = 16
SSM_HEAD = 64
SSM_DIM = 1024
SSM_GROUPS = 2
SSM_STATE = 128
SSM_CONV = 5
XBC_DIM = 1536
MIX_DIM = 1536

MLA_HEADS = 16
MLA_Q_LORA = 384
MLA_KV_LORA = 256
MLA_NOPE = 64
MLA_ROPE = 32
MLA_V = 64
ROPE_THETA = 10000.0

N_EXPERTS = 64
TOP_K = 6
N_EXPERT_GROUPS = 8
TOPK_GROUPS = 4
EXPERT_FF = 256
ROUTED_SCALE = 2.5

SUBLANES = 8
LANES = 128
HALO = SUBLANES
VMEM_LIMIT = 56 * 2**20


class Geom:
    def __init__(self, b, t, c):
        self.b, self.t, self.c = b, t, c
        self.nl, self.nc = b * t, b * c
        self.n = self.nl + self.nc


def _row_tile(g, want, within_seq=False):
    tile = math.gcd(want, g.t)
    if within_seq:
        tile = math.gcd(tile, g.c)
    assert g.nc % tile == 0 and tile % SUBLANES == 0
    return tile


def _cparams(*sem):
    return pltpu.CompilerParams(dimension_semantics=sem, vmem_limit_bytes=VMEM_LIMIT)


def _full(shape):
    nd = len(shape)
    return pl.BlockSpec(shape, lambda *_: (0,) * nd)


def _bdot(a, b):
    return jnp.dot(a.astype(BF16), b.astype(BF16), preferred_element_type=F32)


def _split3(a):
    a1 = a.astype(BF16)
    r = a - a1.astype(F32)
    a2 = r.astype(BF16)
    r = r - a2.astype(F32)
    return a1, a2, r.astype(BF16)


def _xdot_l(a, m):
    return sum(jnp.dot(p, m, preferred_element_type=F32) for p in _split3(a))


def _xdot_r(m, a):
    return sum(jnp.dot(m, p, preferred_element_type=F32) for p in _split3(a))


def _sigmoid(x):
    return 0.5 * (1.0 + jnp.tanh(0.5 * x))


def _silu(x):
    return x * _sigmoid(x)


def _softplus(x):
    return jnp.maximum(x, 0.0) + jnp.log(1.0 + jnp.exp(-jnp.abs(x)))


def _seq_pos(r, tile, g):
    n_lat = g.nl // tile
    per_l, per_c = g.t // tile, g.c // tile
    is_lat = r < n_lat
    pos = jnp.where(is_lat, r % per_l, (r - n_lat) % per_c)
    per = jnp.where(is_lat, per_l, per_c)
    return pos == 0, pos == per - 1


def _mod_spec(g, tile, col):
    return pl.BlockSpec((1, 1, D_MODEL), lambda r, *_: (jnp.minimum(r * tile // g.t, g.b), 0, col))


def _halo_specs(g, tile, width):
    nb = g.n // HALO
    per = tile // HALO
    prev = pl.BlockSpec((HALO, width), lambda r: (jnp.maximum(r * per - 1, 0), 0))
    nxt = pl.BlockSpec((HALO, width), lambda r: (jnp.minimum((r + 1) * per, nb - 1), 0))
    return prev, nxt


def _fill_halo(scr, cur_ref, prev_ref, next_ref, first, last, tile):
    scr[pl.ds(0, HALO), :] = prev_ref[...] * jnp.where(first, 0.0, 1.0)
    scr[pl.ds(HALO, tile), :] = cur_ref[...]
    scr[pl.ds(HALO + tile, HALO), :] = next_ref[...] * jnp.where(last, 0.0, 1.0)


def _mods_kernel(c_ref, w_ref, b_ref, o_ref):
    o_ref[0] = _bdot(_silu(c_ref[...]), w_ref[0]) + b_ref[0]


def ada_mods(cond, ada_w, ada_b):
    depth, d, n6 = ada_w.shape
    rows = cond.shape[0]
    tn = 1536
    return pl.pallas_call(
        _mods_kernel,
        grid=(depth, n6 // tn),
        in_specs=[pl.BlockSpec((rows, d), lambda l, j: (0, 0)),
                  pl.BlockSpec((1, d, tn), lambda l, j: (l, 0, j)),
                  pl.BlockSpec((1, 1, tn), lambda l, j: (l, 0, j))],
        out_specs=pl.BlockSpec((1, rows, tn), lambda l, j: (l, 0, j)),
        out_shape=jax.ShapeDtypeStruct((depth, rows, n6), F32),
        compiler_params=_cparams("arbitrary", "arbitrary"),
        name="ada_mods",
    )(cond, ada_w, ada_b.reshape(depth, 1, n6))


def _normmod(x, gain, shift, scale):
    ms = jnp.mean(x * x, axis=-1, keepdims=True)
    h = x * lax.rsqrt(ms + NORM_EPS) * gain
    return h * (1.0 + scale) + shift


def _normmod_mm_kernel(x_ref, g_ref, sh_ref, sc_ref, *refs):
    nw = len(refs) // 2
    h = _normmod(x_ref[...], g_ref[...], sh_ref[0], sc_ref[0]).astype(BF16)
    for w_ref, o_ref in zip(refs[:nw], refs[nw:]):
        o_ref[...] = jnp.dot(h, w_ref[...], preferred_element_type=F32).astype(o_ref.dtype)


def normmod_mm(g, xa, nrows, gain, mods, shift_col, scale_col, weights, tile=256):
    d = xa.shape[1]
    tile = _row_tile(g, tile)
    return pl.pallas_call(
        _normmod_mm_kernel,
        grid=(nrows // tile,),
        in_specs=[pl.BlockSpec((tile, d), lambda r: (r, 0)), _full((1, d)),
                  _mod_spec(g, tile, shift_col), _mod_spec(g, tile, scale_col)]
                 + [_full(w.shape) for w in weights],
        out_specs=[pl.BlockSpec((tile, w.shape[1]), lambda r: (r, 0)) for w in weights],
        out_shape=[jax.ShapeDtypeStruct((nrows, w.shape[1]), F32) for w in weights],
        compiler_params=_cparams("arbitrary"),
        name="normmod_mm",
    )(xa, gain.reshape(1, d), mods, mods, *weights)


def _head_ones():
    i = np.arange(RW_DIM) // RW_HEAD
    return jnp.asarray(i[:, None] == i[None, :], BF16)


def _rw_prep_kernel(g, tile, cur_ref, prev_ref, next_ref, mu_ref, w0_ref, w2_ref, a0_ref, a2_ref, g2_ref,
                    kk_ref, ones_ref, r_out, k_out, v_out, kk_out, w_out, a_out, g_out, scr):
    first, last = _seq_pos(pl.program_id(0), tile, g)
    _fill_halo(scr, cur_ref, prev_ref, next_ref, first, last, tile)
    x = cur_ref[...]
    nb = 0.5 * (scr[pl.ds(HALO - 1, tile), :] + scr[pl.ds(HALO + 1, tile), :])
    mixed = x + mu_ref[...] * (nb - x)
    r = mixed[:, 0:512]
    k = mixed[:, 512:1024]
    v = mixed[:, 1024:1536]
    xw = mixed[:, 1536:1664]
    xa = mixed[:, 1664:1792]
    xg = mixed[:, 1792:1920]
    w_log = -_softplus(-(w0_ref[...] + _bdot(jnp.tanh(xw), w2_ref[...]))) - 0.5
    w_out[...] = jnp.exp(-jnp.exp(w_log))
    a_out[...] = _sigmoid(a0_ref[...] + _bdot(xa, a2_ref[...]))
    g_out[...] = _bdot(_sigmoid(xg), g2_ref[...])
    kkr = k * kk_ref[...]
    ss = _xdot_l(kkr * kkr, ones_ref[...])
    r_out[...] = r
    k_out[...] = k
    v_out[...] = v
    kk_out[...] = kkr * lax.rsqrt(ss + 1e-12)


def _blockdiag2(w):
    z = jnp.zeros_like(w[0])
    return jnp.concatenate([jnp.concatenate([w[0], z], 1), jnp.concatenate([z, w[1]], 1)], 0)


def _seq_major_spec(g, tile, width):
    n_lat, per_l, per_c = g.nl // tile, g.t // tile, g.c // tile

    def index(r):
        lat = (r // per_l) * (per_l + per_c) + per_c + r % per_l
        i = r - n_lat
        return jnp.where(r < n_lat, lat, (i // per_c) * (per_l + per_c) + i % per_c), 0

    return pl.BlockSpec((tile, width), index)


def rw_prep(g, rw, mu, w0, w2, a0, a2, g2, k_k, tile):
    n = g.n
    prev, nxt = _halo_specs(g, tile, RW_COLS)
    row = functools.partial(_seq_major_spec, g, tile)
    shp = lambda w: jax.ShapeDtypeStruct((n, w), F32)
    return pl.pallas_call(
        functools.partial(_rw_prep_kernel, g, tile),
        grid=(n // tile,),
        in_specs=[pl.BlockSpec((tile, RW_COLS), lambda r: (r, 0)), prev, nxt, _full((1, RW_COLS)),
                  _full((1, 1024)), _full((128, 1024)),
                  _full((1, 1024)), _full((128, 1024)), _full((128, 512)), _full((1, 512)), _full((512, 512))],
        out_specs=[row(512), row(512), row(512), row(512), row(1024), row(1024), row(512)],
        out_shape=[shp(512), shp(512), shp(512), shp(512), shp(1024), shp(1024), shp(512)],
        scratch_shapes=[pltpu.VMEM((tile + 2 * HALO, RW_COLS), F32)],
        compiler_params=_cparams("arbitrary"),
        name="rw_prep",
    )(rw, rw, rw, mu.reshape(1, -1), w0.reshape(1, -1), _blockdiag2(w2).astype(BF16), a0.reshape(1, -1),
      _blockdiag2(a2).astype(BF16), g2.astype(BF16), k_k.reshape(1, -1), _head_ones())


KEYS_PER_ITER = 8


def _rw_scan_kernel(tt, r_ref, k_ref, v_ref, kk_ref, w_ref, a_ref, ka_ref, y_ref, s_scr, b_scr, kd_scr):
    d = pl.program_id(0)

    @pl.when(pl.program_id(2) == 0)
    def _():
        s_scr[...] = jnp.zeros_like(s_scr)

    a = a_ref[0]
    b_scr[...] = kk_ref[...] * a
    kd_scr[...] = k_ref[...] * (1.0 + (a - 1.0) * ka_ref[...])
    nv = RW_HEAD // SUBLANES

    blocks = [pl.ds(j * SUBLANES, SUBLANES) for j in range(nv)]
    zeros = tuple(jnp.zeros((SUBLANES, LANES), F32) for _ in range(nv))

    def step(i, carry):
        t = i + d * (tt - 1 - 2 * i)
        vt = [v_ref[t, sl, :] for sl in blocks]

        def read_pass(c, sa):
            sa = list(sa)
            for u in range(KEYS_PER_ITER):
                k = c * KEYS_PER_ITER + u
                kkb = kk_ref[t, pl.ds(k, 1), :]
                for j, sl in enumerate(blocks):
                    sa[j] = sa[j] + s_scr[k, sl, :] * kkb
            return tuple(sa)

        sa = lax.fori_loop(0, RW_HEAD // KEYS_PER_ITER, read_pass, zeros)

        def update_pass(c, y):
            y = list(y)
            for u in range(KEYS_PER_ITER):
                k = c * KEYS_PER_ITER + u
                wb = w_ref[0, t, pl.ds(k, 1), :]
                bb = b_scr[t, pl.ds(k, 1), :]
                kb = kd_scr[t, pl.ds(k, 1), :]
                rb = r_ref[t, pl.ds(k, 1), :]
                for j, sl in enumerate(blocks):
                    new = s_scr[k, sl, :] * wb + (vt[j] * kb - sa[j] * bb)
                    s_scr[k, sl, :] = new
                    y[j] = y[j] + new * rb
            return tuple(y)

        y = lax.fori_loop(0, RW_HEAD // KEYS_PER_ITER, update_pass, zeros)
        for j, sl in enumerate(blocks):
            y_ref[0, t, sl, :] = y[j]
        return carry

    lax.fori_loop(0, tt, step, 0)


def rw_scan(r, k, v, kk, w, a, ka, c_len, tt):
    s_len, _, chains = r.shape
    nb, ncb = s_len // tt, c_len // tt

    def tb(d, s):
        return jnp.where(d == 0, s, jnp.where(s < ncb, ncb - 1 - s, nb - 1 - (s - ncb)))

    shared = pl.BlockSpec((tt, RW_HEAD, LANES), lambda d, c, s: (tb(d, s), 0, c))
    per_dir = pl.BlockSpec((1, tt, RW_HEAD, LANES), lambda d, c, s: (d, tb(d, s), 0, c))
    return pl.pallas_call(
        functools.partial(_rw_scan_kernel, tt),
        grid=(2, chains // LANES, nb),
        in_specs=[shared, shared, shared, shared, per_dir, per_dir,
                  pl.BlockSpec((RW_HEAD, LANES), lambda d, c, s: (0, c))],
        out_specs=per_dir,
        out_shape=jax.ShapeDtypeStruct((2, s_len, RW_HEAD, chains), F32),
        scratch_shapes=[pltpu.VMEM((RW_HEAD, RW_HEAD, LANES), F32), pltpu.VMEM((tt, RW_HEAD, LANES), F32),
                        pltpu.VMEM((tt, RW_HEAD, LANES), F32)],
        compiler_params=_cparams("arbitrary", "arbitrary", "arbitrary"),
        name="rw_scan",
    )(r, k, v, kk, w, a, ka)


def _to_chains(b, u):
    s = u.shape[0] // b
    return u.reshape(b, s, RW_HEADS, RW_HEAD).transpose(1, 3, 0, 2).reshape(s, RW_HEAD, b * RW_HEADS)


def _to_chains2(b, u):
    s = u.shape[0] // b
    return u.reshape(b, s, 2, RW_HEADS, RW_HEAD).transpose(2, 1, 4, 0, 3).reshape(2, s, RW_HEAD, b * RW_HEADS)


def _from_chains2(b, y):
    s = y.shape[1]
    return y.reshape(2, s, RW_HEAD, b, RW_HEADS).transpose(3, 1, 0, 4, 2).reshape(b * s, 2 * RW_DIM)


def _conv_kernel(g, tile, cur_ref, prev_ref, next_ref, w_ref, b_ref, o_ref, scr):
    first, last = _seq_pos(pl.program_id(0), tile, g)
    _fill_halo(scr, cur_ref, prev_ref, next_ref, first, last, tile)
    acc = b_ref[...] + w_ref[pl.ds(0, 1), :] * scr[pl.ds(HALO - 2, tile), :]
    for j in range(1, SSM_CONV):
        acc = acc + w_ref[pl.ds(j, 1), :] * scr[pl.ds(HALO - 2 + j, tile), :]
    o_ref[...] = _silu(acc)


def ssm_conv(g, xbc, conv_w, conv_b, tile):
    prev, nxt = _halo_specs(g, tile, XBC_DIM)
    row = pl.BlockSpec((tile, XBC_DIM), lambda r: (r, 0))
    return pl.pallas_call(
        functools.partial(_conv_kernel, g, tile),
        grid=(g.n // tile,),
        in_specs=[row, prev, nxt, _full((SSM_CONV, XBC_DIM)), _full((1, XBC_DIM))],
        out_specs=row,
        out_shape=jax.ShapeDtypeStruct((g.n, XBC_DIM), F32),
        scratch_shapes=[pltpu.VMEM((tile + 2 * HALO, XBC_DIM), F32)],
        compiler_params=_cparams("arbitrary"),
        name="ssm_conv",
    )(xbc, xbc, xbc, conv_w, conv_b.reshape(1, -1))


def _ssd_dir(rev, xbc, dtr, bias, aneg, expand, tri, st_ref):
    cl = xbc.shape[0]
    xs = xbc[:, 0:SSM_DIM]
    dt = _softplus(dtr + bias)
    da = dt * aneg
    cs = _xdot_r(tri, da)
    dt_x = _xdot_l(dt, expand)
    cs_x = _xdot_l(cs, expand)
    cs_end = cs_x[0:1, :] if rev else cs_x[cl - 1:cl, :]
    xdt = xs * dt_x
    cs_t = cs.T
    row = lax.broadcasted_iota(jnp.int32, (cl, cl), 0)
    col = lax.broadcasted_iota(jnp.int32, (cl, cl), 1)
    keep = (col >= row) if rev else (col <= row)
    lane = lax.broadcasted_iota(jnp.int32, (cl, LANES), 1)
    per_group = SSM_DIM // SSM_GROUPS
    y_parts = []
    for gi in range(SSM_GROUPS):
        bm = xbc[:, SSM_DIM + gi * SSM_STATE:SSM_DIM + (gi + 1) * SSM_STATE]
        cm = xbc[:, SSM_DIM + SSM_GROUPS * SSM_STATE + gi * SSM_STATE:
                 SSM_DIM + SSM_GROUPS * SSM_STATE + (gi + 1) * SSM_STATE]
        cb = lax.dot_general(cm.astype(BF16), bm.astype(BF16), (((1,), (1,)), ((), ())),
                             preferred_element_type=F32)
        gs = slice(gi * per_group, (gi + 1) * per_group)
        s_in = st_ref[:, gs]
        y_off = _bdot(cm, s_in) * jnp.exp(cs_x[:, gs])
        for p in range(per_group // LANES):
            h0 = (gi * per_group + p * LANES) // SSM_HEAD
            x_pair = xdt[:, h0 * SSM_HEAD:h0 * SSM_HEAD + LANES].astype(BF16)
            ys = []
            for hh in (h0, h0 + 1):
                seg = cs[:, hh:hh + 1] - cs_t[hh:hh + 1, :]
                m = cb * jnp.exp(jnp.where(keep, seg, -jnp.inf))
                ys.append(jnp.dot(m.astype(BF16), x_pair, preferred_element_type=F32))
            y_parts.append(jnp.where(lane < SSM_HEAD, ys[0], ys[1]) + y_off[:, p * LANES:(p + 1) * LANES])
        xd = (xdt[:, gs] * jnp.exp(cs_end[:, gs] - cs_x[:, gs])).astype(BF16)
        st_new = lax.dot_general(bm.astype(BF16), xd, (((0,), (0,)), ((), ())), preferred_element_type=F32)
        st_ref[:, gs] = s_in * jnp.exp(cs_end[:, gs]) + st_new
    return jnp.concatenate(y_parts, axis=1)


def _ssd_kernel(xf_ref, dtf_ref, xb_ref, dtb_ref, bias_ref, aneg_ref, exp_ref, tril_ref, triu_ref,
                yf_ref, yb_ref, st_scr):
    @pl.when(pl.program_id(1) == 0)
    def _():
        st_scr[...] = jnp.zeros_like(st_scr)

    yf_ref[...] = _ssd_dir(False, xf_ref[...], dtf_ref[...], bias_ref[0], aneg_ref[0], exp_ref[...],
                           tril_ref[...], st_scr.at[0])
    yb_ref[...] = _ssd_dir(True, xb_ref[...], dtb_ref[...], bias_ref[1], aneg_ref[1], exp_ref[...],
                           triu_ref[...], st_scr.at[1])


def ssd_scan(g, xbc_act, dt_f, dt_b, dt_bias, a_log, cl):
    ncc, nlc = g.c // cl, g.t // cl
    ns = ncc + nlc

    def rows(b, c):
        return jnp.where(c < ncc, g.nl // cl + b * ncc + c, b * nlc + (c - ncc))

    def fwd(b, s):
        return rows(b, s)

    def bwd(b, s):
        return rows(b, jnp.where(s < ncc, ncc - 1 - s, ns - 1 - (s - ncc)))

    pad = lambda u: jnp.pad(u, ((0, 0), (0, LANES - SSM_HEADS))).reshape(2, 1, LANES)
    expand = jnp.asarray(np.arange(LANES)[:, None] == (np.arange(SSM_DIM) // SSM_HEAD)[None, :], BF16)
    idx = np.arange(cl)
    tril = jnp.asarray(idx[:, None] >= idx[None, :], BF16)
    triu = jnp.asarray(idx[:, None] <= idx[None, :], BF16)
    spec = lambda w, f: pl.BlockSpec((cl, w), lambda b, s: (f(b, s), 0))
    return pl.pallas_call(
        _ssd_kernel,
        grid=(g.b, ns),
        in_specs=[spec(XBC_DIM, fwd), spec(LANES, fwd), spec(XBC_DIM, bwd), spec(LANES, bwd),
                  _full((2, 1, LANES)), _full((2, 1, LANES)), _full((LANES, SSM_DIM)), _full((cl, cl)),
                  _full((cl, cl))],
        out_specs=[spec(SSM_DIM, fwd), spec(SSM_DIM, bwd)],
        out_shape=[jax.ShapeDtypeStruct((g.n, SSM_DIM), F32)] * 2,
        scratch_shapes=[pltpu.VMEM((2, SSM_STATE, SSM_DIM), F32)],
        compiler_params=_cparams("arbitrary", "arbitrary"),
        name="ssd_scan",
    )(xbc_act, dt_f, xbc_act, dt_b, pad(dt_bias), pad(-jnp.exp(a_log)), expand, tril, triu)


def _even_out_kernel(x_ref, gate_ref, y2_ref, r_ref, k_ref, v_ref, a_ref, g_ref,
                     ysf_ref, ysb_ref, xs_ref, z_ref, ka_ref, rk_ref, gnw_ref, gnb_ref, ones_ref,
                     dsk_ref, nw_ref, wo_rw_ref, wo_ss_ref, o_ref):
    ones = ones_ref[...]
    y = y2_ref[:, 0:RW_DIM] + y2_ref[:, RW_DIM:]
    mean = _xdot_l(y, ones) * (1.0 / RW_HEAD)
    dev = y - mean
    var = _xdot_l(dev * dev, ones) * (1.0 / RW_HEAD)
    yn = dev * lax.rsqrt(var + RW_GN_EPS) * gnw_ref[...] + gnb_ref[...]
    r, k, a, ka = r_ref[...], k_ref[...], a_ref[...], ka_ref[...]
    ksum = k * (1.0 + (a[:, 0:RW_DIM] - 1.0) * ka) + k * (1.0 + (a[:, RW_DIM:] - 1.0) * ka)
    bonus = _xdot_l(r * ksum * rk_ref[...], ones) * v_ref[...]
    out_rw = (yn + bonus) * g_ref[...]

    ys = ysf_ref[...] + ysb_ref[...] + xs_ref[...] * dsk_ref[...]
    ys = ys * _silu(z_ref[...])
    half = SSM_DIM // SSM_GROUPS
    parts = []
    for gi in range(SSM_GROUPS):
        yg = ys[:, gi * half:(gi + 1) * half]
        parts.append(yg * lax.rsqrt(jnp.mean(yg * yg, axis=-1, keepdims=True) + NORM_EPS))
    out_ss = jnp.concatenate(parts, axis=1) * nw_ref[...]
    o = _bdot(out_rw, wo_rw_ref[...]) + _bdot(out_ss, wo_ss_ref[...])
    o_ref[...] = x_ref[...] + gate_ref[0] * o


def even_out(g, xa, mods, y2, r, k, v, a, gg, ys_f, ys_b, xbc_act, z, ka, rk, gn_w, gn_b,
             d_skip, norm_w, w_out, tile=256):
    tile = _row_tile(g, tile, within_seq=True)
    row = lambda w: pl.BlockSpec((tile, w), lambda i: (i, 0))
    seq = functools.partial(_seq_major_spec, g, tile)
    vec = lambda w: _full((1, w))
    return pl.pallas_call(
        _even_out_kernel,
        grid=(g.n // tile,),
        in_specs=[row(D_MODEL), _mod_spec(g, tile, 2), seq(1024), seq(512), seq(512), seq(512),
                  seq(1024), seq(512), row(1024), row(1024), row(1024), row(1024),
                  vec(512), vec(512), vec(512), vec(512), _full((512, 512)), vec(1024), vec(1024),
                  _full((RW_DIM, D_MODEL)), _full((SSM_DIM, D_MODEL))],
        out_specs=row(D_MODEL),
        out_shape=jax.ShapeDtypeStruct((g.n, D_MODEL), F32),
        compiler_params=_cparams("arbitrary"),
        name="even_out",
    )(xa, mods, y2, r, k, v, a, gg, ys_f, ys_b, xbc_act, z, ka.reshape(1, -1), rk.reshape(1, -1),
      gn_w.reshape(1, -1), gn_b.reshape(1, -1), _head_ones(), jnp.repeat(d_skip, SSM_HEAD).reshape(1, -1),
      norm_w.reshape(1, -1), w_out[:RW_DIM].astype(BF16), w_out[RW_DIM:].astype(BF16))


def even_layer(g, xa, mods, norm_g, w_in, w_out, mu, w0, w2, a0, a2, g2, k_k, k_a, r_k, gn_w, gn_b,
               conv_w, conv_b, dt_bias, a_log, d_skip, ssm_norm_w):
    o_z, o_x, o_dt = RW_COLS, RW_COLS + SSM_DIM, RW_COLS + SSM_DIM + XBC_DIM
    padw = lambda w: jnp.pad(w, ((0, 0), (0, LANES - w.shape[1])))
    weights = [w_in[:, :o_z], w_in[:, o_z:o_x], w_in[:, o_x:o_dt], padw(w_in[:, o_dt:o_dt + SSM_HEADS]),
               padw(w_in[:, o_dt + SSM_HEADS:])]
    rw, z, xbc, dt_f, dt_b = normmod_mm(g, xa, g.n, norm_g, mods, 0, 1, [w.astype(BF16) for w in weights])

    seq_tile = _row_tile(g, 256, within_seq=True)
    r, k, v, kk, w, a, gg = rw_prep(g, rw, mu, w0, w2, a0, a2, g2, k_k, seq_tile)
    ka_c = jnp.tile(k_a.reshape(RW_HEADS, RW_HEAD).T, (1, g.b))
    one = functools.partial(_to_chains, g.b)
    two = functools.partial(_to_chains2, g.b)
    y2 = _from_chains2(g.b, rw_scan(one(r), one(k), one(v), one(kk), two(w), two(a), ka_c, g.c,
                                    math.gcd(32, g.c)))

    xbc_act = ssm_conv(g, xbc, conv_w, conv_b, seq_tile)
    ys_f, ys_b = ssd_scan(g, xbc_act, dt_f, dt_b, dt_bias, a_log, math.gcd(128, g.c))
    return even_out(g, xa, mods, y2, r, k, v, a, gg, ys_f, ys_b, xbc_act, z, k_a, r_k, gn_w, gn_b,
                    d_skip, ssm_norm_w, w_out)


def _router_kernel(x_ref, g_ref, sh_ref, sc_ref, rw_ref, rb_ref, h_ref, gates_ref):
    h = _normmod(x_ref[...], g_ref[...], sh_ref[0], sc_ref[0])
    h_ref[...] = h.astype(BF16)
    nt = (((1,), (1,)), ((), ()))
    h3, w3 = _split3(h), _split3(rw_ref[...])
    logits = sum(lax.dot_general(w3[i], h3[j], nt, preferred_element_type=F32)
                 for i in range(3) for j in range(3 - i))
    scores = _sigmoid(logits)
    sel = scores + rb_ref[...]
    tile = sel.shape[1]
    per = N_EXPERTS // N_EXPERT_GROUPS
    grp = sel.reshape(N_EXPERT_GROUPS, per, tile)
    pos = lax.broadcasted_iota(jnp.int32, grp.shape, 1).astype(F32)
    m1 = jnp.max(grp, axis=1, keepdims=True)
    first = jnp.min(jnp.where(grp == m1, pos, float(per)), axis=1, keepdims=True)
    m2 = jnp.max(jnp.where(pos == first, -jnp.inf, grp), axis=1, keepdims=True)
    gsc = jnp.broadcast_to(m1 + m2, grp.shape)
    gid = lax.broadcasted_iota(jnp.int32, grp.shape, 0)
    beaten = jnp.zeros(grp.shape, F32)
    for o in range(N_EXPERT_GROUPS):
        other = gsc[o:o + 1]
        beaten = beaten + jnp.where((other > gsc) | ((other == gsc) & (o < gid)), 1.0, 0.0)
    cand = jnp.where(beaten < TOPK_GROUPS, grp, -jnp.inf).reshape(N_EXPERTS, tile)
    eid = lax.broadcasted_iota(jnp.int32, cand.shape, 0).astype(F32)
    chosen = jnp.zeros(cand.shape, F32)
    for _ in range(TOP_K):
        best = jnp.max(cand, axis=0, keepdims=True)
        pick = eid == jnp.min(jnp.where(cand == best, eid, float(N_EXPERTS)), axis=0, keepdims=True)
        chosen = jnp.where(pick, 1.0, chosen)
        cand = jnp.where(pick, -jnp.inf, cand)
    wts = scores * chosen
    gates = wts / jnp.sum(wts, axis=0, keepdims=True) * ROUTED_SCALE
    gates_ref[...] = jnp.concatenate([gates, jnp.zeros_like(gates)], axis=0).T


def moe_router(g, xa, nrows, gain, mods, router_w, router_bias, tile=512):
    d = xa.shape[1]
    tile = _row_tile(g, tile)
    return pl.pallas_call(
        _router_kernel,
        grid=(nrows // tile,),
        in_specs=[pl.BlockSpec((tile, d), lambda r: (r, 0)), _full((1, d)), _mod_spec(g, tile, 3),
                  _mod_spec(g, tile, 4), _full((N_EXPERTS, d)), _full((N_EXPERTS, 1))],
        out_specs=[pl.BlockSpec((tile, d), lambda r: (r, 0)), pl.BlockSpec((tile, LANES), lambda r: (r, 0))],
        out_shape=[jax.ShapeDtypeStruct((nrows, d), BF16), jax.ShapeDtypeStruct((nrows, LANES), F32)],
        compiler_params=_cparams("arbitrary"),
        name="moe_router",
    )(xa, gain.reshape(1, d), mods, mods, router_w.T, router_bias.reshape(N_EXPERTS, 1))


EXPERTS_PER_STEP = 4


def _experts_kernel(final, h_ref, gates_ref, w1_ref, w3_ref, w2_ref, sw1_ref, sw3_ref, sw2_ref, x_ref, gate_ref,
                    fin_ref, o_ref, acc):
    e = pl.program_id(1)
    h = h_ref[...]
    ff = EXPERT_FF

    def ffn(w1, w3):
        return _silu(jnp.dot(h, w1, preferred_element_type=F32)) * jnp.dot(h, w3, preferred_element_type=F32)

    @pl.when(e == 0)
    def _():
        acc[...] = jnp.dot(ffn(sw1_ref[...], sw3_ref[...]).astype(BF16), sw2_ref[...],
                           preferred_element_type=F32)

    first = e * EXPERTS_PER_STEP
    mine = pltpu.roll(gates_ref[...], jnp.where(first == 0, 0, LANES - first), axis=1)
    hid = jnp.concatenate([ffn(w1_ref[j], w3_ref[j]) * mine[:, j:j + 1] for j in range(EXPERTS_PER_STEP)], axis=1)
    acc[...] += jnp.dot(hid.astype(BF16), w2_ref[0], preferred_element_type=F32)

    @pl.when(e == pl.num_programs(1) - 1)
    def _():
        out = x_ref[...] + gate_ref[0] * acc[...]
        if final:
            ms = jnp.mean(out * out, axis=-1, keepdims=True)
            out = out * lax.rsqrt(ms + NORM_EPS) * fin_ref[...]
        o_ref[...] = out


def moe_experts(g, xa, nrows, mods, h, gates, w1, w3, w2, sw1, sw3, sw2, final_gain, final, tile=1024):
    d = xa.shape[1]
    tile = _row_tile(g, tile)
    eps = EXPERTS_PER_STEP
    w2c = w2.astype(BF16).reshape(N_EXPERTS // eps, eps * EXPERT_FF, d)
    up = pl.BlockSpec((eps, d, EXPERT_FF), lambda i, e: (e, 0, 0))
    return pl.pallas_call(
        functools.partial(_experts_kernel, final),
        grid=(nrows // tile, N_EXPERTS // eps),
        in_specs=[pl.BlockSpec((tile, d), lambda i, e: (i, 0)),
                  pl.BlockSpec((tile, LANES), lambda i, e: (i, 0)),
                  up, up, pl.BlockSpec((1, eps * EXPERT_FF, d), lambda i, e: (e, 0, 0)),
                  _full(sw1.shape), _full(sw3.shape), _full(sw2.shape),
                  pl.BlockSpec((tile, d), lambda i, e: (i, 0)), _mod_spec(g, tile, 5), _full((1, d))],
        out_specs=pl.BlockSpec((tile, d), lambda i, e: (i, 0)),
        out_shape=jax.ShapeDtypeStruct((nrows, d), F32),
        scratch_shapes=[pltpu.VMEM((tile, d), F32)],
        compiler_params=_cparams("arbitrary", "arbitrary"),
        name="moe_experts",
    )(h, gates, w1.astype(BF16), w3.astype(BF16), w2c, sw1.astype(BF16), sw3.astype(BF16), sw2.astype(BF16),
      xa, mods, final_gain.reshape(1, d))


def moe_layer(g, xa, nrows, mods, gain, router_w, router_bias, w1, w3, w2, sw1, sw3, sw2, final_gain, final):
    h, gates = moe_router(g, xa, nrows, gain, mods, router_w, router_bias)
    return moe_experts(g, xa, nrows, mods, h, gates, w1, w3, w2, sw1, sw3, sw2, final_gain, final)


HEAD_PAD = 128


def _rope_tables(g, scale):
    rows = g.t // GRID_W
    r_idx = jnp.repeat(jnp.arange(rows, dtype=F32), GRID_W)
    c_idx = jnp.tile(jnp.arange(GRID_W, dtype=F32), rows)
    axis_dim = MLA_ROPE // 2
    inv_freq = ROPE_THETA ** (-jnp.arange(0, axis_dim, 2, dtype=F32) / axis_dim)
    ang = jnp.concatenate([r_idx[:, None] * inv_freq, c_idx[:, None] * inv_freq], axis=-1)
    cos, sin = jnp.cos(ang), jnp.sin(ang)
    cos2 = jnp.concatenate([cos, cos], axis=1)
    sin2 = jnp.concatenate([-sin, sin], axis=1)
    t = g.t
    q_cos = jnp.concatenate([jnp.ones((t, MLA_NOPE), F32), cos2, jnp.zeros((t, 32), F32)], axis=1) * scale
    q_sin = jnp.concatenate([jnp.zeros((t, MLA_NOPE), F32), sin2, jnp.zeros((t, 32), F32)], axis=1) * scale
    zpad = jnp.zeros((t, LANES - MLA_ROPE), F32)
    k_cos = jnp.concatenate([cos2, zpad], axis=1)
    k_sin = jnp.concatenate([sin2, zpad], axis=1)
    c_cos = jnp.concatenate([jnp.ones((g.c, MLA_ROPE), F32), jnp.zeros((g.c, LANES - MLA_ROPE), F32)], axis=1)
    k_cos = jnp.concatenate([k_cos, c_cos], axis=0)
    k_sin = jnp.concatenate([k_sin, jnp.zeros((g.c, LANES), F32)], axis=0)
    return q_cos, q_sin, k_cos, k_sin


def _q_kernel(cq_ref, qn_ref, w1_ref, w2_ref, cos_ref, sin_ref, q_ref):
    cq = cq_ref[...]
    qn = (cq * lax.rsqrt(jnp.mean(cq * cq, axis=-1, keepdims=True) + NORM_EPS) * qn_ref[...]).astype(BF16)
    q1 = jnp.dot(qn, w1_ref[...], preferred_element_type=F32)
    q2 = jnp.dot(qn, w2_ref[...], preferred_element_type=F32)
    cos, sin = cos_ref[...], sin_ref[...]
    for h in range(MLA_HEADS):
        sl = slice(h * HEAD_PAD, (h + 1) * HEAD_PAD)
        q_ref[:, sl] = (q1[:, sl] * cos + q2[:, sl] * sin).astype(BF16)


def mla_queries(g, cq, q_norm, w1, w2, q_cos, q_sin, tile):
    per = g.t // tile
    width = MLA_HEADS * HEAD_PAD
    return pl.pallas_call(
        _q_kernel,
        grid=(g.nl // tile,),
        in_specs=[pl.BlockSpec((tile, MLA_Q_LORA), lambda r: (r, 0)), _full((1, MLA_Q_LORA)),
                  _full(w1.shape), _full(w2.shape),
                  pl.BlockSpec((tile, HEAD_PAD), lambda r: (r % per, 0)),
                  pl.BlockSpec((tile, HEAD_PAD), lambda r: (r % per, 0))],
        out_specs=pl.BlockSpec((tile, width), lambda r: (r, 0)),
        out_shape=jax.ShapeDtypeStruct((g.nl, width), BF16),
        compiler_params=_cparams("arbitrary"),
        name="mla_queries",
    )(cq, q_norm.reshape(1, -1), w1, w2, q_cos, q_sin)


def _kv_kernel(ckv_ref, kr1_ref, kr2_ref, kvn_ref, wk_ref, wv_ref, place_ref, cos_ref, sin_ref, k_ref, v_ref):
    ckv = ckv_ref[...]
    kvn = (ckv * lax.rsqrt(jnp.mean(ckv * ckv, axis=-1, keepdims=True) + NORM_EPS) * kvn_ref[...]).astype(BF16)
    kr = (kr1_ref[...] * cos_ref[...] + kr2_ref[...] * sin_ref[...]).astype(BF16)
    k = jnp.dot(kvn, wk_ref[...], preferred_element_type=F32) + jnp.dot(kr, place_ref[...], preferred_element_type=F32)
    k_ref[...] = k.astype(BF16)
    v_ref[...] = jnp.dot(kvn, wv_ref[...], preferred_element_type=F32).astype(BF16)


def mla_keys_values(g, ckv, kr1, kr2, kv_norm, wk, wv, place, k_cos, k_sin, tile):
    n_lat, per_l, per_c = g.nl // tile, g.t // tile, g.c // tile

    def tab(r):
        return (jnp.where(r < n_lat, r % per_l, per_l + (r - n_lat) % per_c), 0)

    row = lambda w: pl.BlockSpec((tile, w), lambda r: (r, 0))
    kw = MLA_HEADS * HEAD_PAD
    return pl.pallas_call(
        _kv_kernel,
        grid=(g.n // tile,),
        in_specs=[row(MLA_KV_LORA), row(LANES), row(LANES), _full((1, MLA_KV_LORA)), _full(wk.shape),
                  _full(wv.shape), _full(place.shape), pl.BlockSpec((tile, LANES), tab),
                  pl.BlockSpec((tile, LANES), tab)],
        out_specs=[row(kw), row(MLA_HEADS * MLA_V)],
        out_shape=[jax.ShapeDtypeStruct((g.n, kw), BF16), jax.ShapeDtypeStruct((g.n, MLA_HEADS * MLA_V), BF16)],
        compiler_params=_cparams("arbitrary"),
        name="mla_keys_values",
    )(ckv, kr1, kr2, kv_norm.reshape(1, -1), wk, wv, place, k_cos, k_sin)


def _attn_kernel(q_ref, kl_ref, kc_ref, vl_ref, vc_ref, o_ref):
    nt = (((1,), (1,)), ((), ()))
    vl, vc = vl_ref[...], vc_ref[...]
    outs = []
    for hh in range(2):
        sl = slice(hh * HEAD_PAD, (hh + 1) * HEAD_PAD)
        q = q_ref[:, sl]
        s_l = lax.dot_general(q, kl_ref[:, sl], nt, preferred_element_type=F32)
        s_c = lax.dot_general(q, kc_ref[:, sl], nt, preferred_element_type=F32)
        m = jnp.maximum(jnp.max(s_l, axis=-1, keepdims=True), jnp.max(s_c, axis=-1, keepdims=True))
        p_l, p_c = jnp.exp(s_l - m), jnp.exp(s_c - m)
        den = jnp.sum(p_l, axis=-1, keepdims=True) + jnp.sum(p_c, axis=-1, keepdims=True)
        o = (jnp.dot(p_l.astype(BF16), vl, preferred_element_type=F32)
             + jnp.dot(p_c.astype(BF16), vc, preferred_element_type=F32))
        outs.append(o / den)
    lane = lax.broadcasted_iota(jnp.int32, outs[0].shape, 1)
    o_ref[...] = jnp.where(lane < MLA_V, outs[0], outs[1]).astype(o_ref.dtype)


def mla_attention(g, q, k, v, tq):
    nq = g.t // tq
    pairs = MLA_HEADS // 2
    pw = 2 * HEAD_PAD
    ctx0 = g.nl // g.c
    return pl.pallas_call(
        _attn_kernel,
        grid=(g.b, pairs, nq),
        in_specs=[pl.BlockSpec((tq, pw), lambda b, p, i: (b * nq + i, p)),
                  pl.BlockSpec((g.t, pw), lambda b, p, i: (b, p)),
                  pl.BlockSpec((g.c, pw), lambda b, p, i: (ctx0 + b, p)),
                  pl.BlockSpec((g.t, 2 * MLA_V), lambda b, p, i: (b, p)),
                  pl.BlockSpec((g.c, 2 * MLA_V), lambda b, p, i: (ctx0 + b, p))],
        out_specs=pl.BlockSpec((tq, 2 * MLA_V), lambda b, p, i: (b * nq + i, p)),
        out_shape=jax.ShapeDtypeStruct((g.nl, MLA_HEADS * MLA_V), BF16),
        compiler_params=_cparams("arbitrary", "arbitrary", "arbitrary"),
        name="mla_attention",
    )(q, k, k, v, v)


def _proj_resid_kernel(x_ref, gate_ref, a_ref, w_ref, o_ref):
    o_ref[...] = x_ref[...] + gate_ref[0] * jnp.dot(a_ref[...], w_ref[...], preferred_element_type=F32)


def proj_resid(g, xa, nrows, mods, act, w, tile=512):
    d = xa.shape[1]
    tile = _row_tile(g, tile)
    return pl.pallas_call(
        _proj_resid_kernel,
        grid=(nrows // tile,),
        in_specs=[pl.BlockSpec((tile, d), lambda r: (r, 0)), _mod_spec(g, tile, 2),
                  pl.BlockSpec((tile, act.shape[1]), lambda r: (r, 0)), _full(w.shape)],
        out_specs=pl.BlockSpec((tile, d), lambda r: (r, 0)),
        out_shape=jax.ShapeDtypeStruct((nrows, d), F32),
        compiler_params=_cparams("arbitrary"),
        name="proj_resid",
    )(xa, mods, act, w)


def mla_layer(g, xa, mods, norm_g, w_in, q_norm, q_up, kv_norm, kv_up, w_out):
    scale = (MLA_NOPE + MLA_ROPE) ** -0.5
    half = MLA_ROPE // 2
    o_kv, o_kr = MLA_Q_LORA, MLA_Q_LORA + MLA_KV_LORA
    w_kr = w_in[:, o_kr:]
    padw = lambda w: jnp.pad(w, ((0, 0), (0, LANES - w.shape[1])))
    weights = [w_in[:, :o_kv], w_in[:, o_kv:o_kr], padw(w_kr),
               padw(jnp.concatenate([w_kr[:, half:], w_kr[:, :half]], axis=1))]
    cq, ckv, kr1, kr2 = normmod_mm(g, xa, g.n, norm_g, mods, 0, 1, [w.astype(BF16) for w in weights])

    qu = q_up.reshape(MLA_Q_LORA, MLA_HEADS, MLA_NOPE + MLA_ROPE)
    zq = jnp.zeros((MLA_Q_LORA, MLA_HEADS, HEAD_PAD - MLA_NOPE - MLA_ROPE), F32)
    w1 = jnp.concatenate([qu, zq], axis=2).reshape(MLA_Q_LORA, -1).astype(BF16)
    swapped = jnp.concatenate([qu[:, :, MLA_NOPE + half:], qu[:, :, MLA_NOPE:MLA_NOPE + half]], axis=2)
    w2 = jnp.concatenate([jnp.zeros((MLA_Q_LORA, MLA_HEADS, MLA_NOPE), F32), swapped, zq], axis=2)
    w2 = w2.reshape(MLA_Q_LORA, -1).astype(BF16)
    kvu = kv_up.reshape(MLA_KV_LORA, MLA_HEADS, MLA_NOPE + MLA_V)
    wk = jnp.concatenate([kvu[:, :, :MLA_NOPE], jnp.zeros((MLA_KV_LORA, MLA_HEADS, HEAD_PAD - MLA_NOPE), F32)],
                         axis=2).reshape(MLA_KV_LORA, -1).astype(BF16)
    wv = kvu[:, :, MLA_NOPE:].reshape(MLA_KV_LORA, -1).astype(BF16)
    col = np.arange(MLA_HEADS * HEAD_PAD)
    place = jnp.asarray((col[None, :] % HEAD_PAD) == (np.arange(LANES)[:, None] + MLA_NOPE), BF16)
    place = place * jnp.asarray(np.arange(LANES)[:, None] < MLA_ROPE, BF16)

    q_cos, q_sin, k_cos, k_sin = _rope_tables(g, scale)
    tile = _row_tile(g, 256, within_seq=True)
    q = mla_queries(g, cq, q_norm, w1, w2, q_cos, q_sin, tile)
    k, v = mla_keys_values(g, ckv, kr1, kr2, kv_norm, wk, wv, place, k_cos, k_sin, tile)
    attn = mla_attention(g, q, k, v, math.gcd(256, g.t))
    return proj_resid(g, xa, g.nl, mods, attn, w_out.astype(BF16))


def kernel(x, c, ctx, c_ctx, ada_w, ada_b, norm_mix, norm_ffn, ev_w_in, ev_w_out, rw_mu, rw_w0, rw_w2, rw_a0, rw_a2, rw_g2, rw_kk, rw_ka, rw_rk, rw_gn_w, rw_gn_b, ssm_conv_w, ssm_conv_b, ssm_dt_bias, ssm_a_log, ssm_d, ssm_norm_w, mla_w_in, mla_q_norm, mla_q_up, mla_kv_norm, mla_kv_up, mla_w_out, router_w, router_bias, exp_w1, exp_w3, exp_w2, sh_w1, sh_w3, sh_w2, final_norm):
    bsz, t, d = x.shape
    g = Geom(bsz, t, ctx.shape[1])
    depth = ada_w.shape[0]
    assert depth == 2 and d == D_MODEL, "layer 0 is the recurrent mixer, layer 1 the final attention layer"

    rows = -(-(bsz + 1) // SUBLANES) * SUBLANES
    cond = jnp.concatenate([c, c_ctx[None, :], jnp.zeros((rows - bsz - 1, d), F32)], axis=0)
    mods = ada_mods(cond, ada_w, ada_b).reshape(depth, rows, 1, 6 * d)
    xa = jnp.concatenate([x.reshape(g.nl, d), ctx.reshape(g.nc, d)], axis=0)

    xa = even_layer(g, xa, mods[0], norm_mix[0], ev_w_in[0], ev_w_out[0], rw_mu[0], rw_w0[0], rw_w2[0], rw_a0[0],
                    rw_a2[0], rw_g2[0], rw_kk[0], rw_ka[0], rw_rk[0], rw_gn_w[0], rw_gn_b[0], ssm_conv_w[0],
                    ssm_conv_b[0], ssm_dt_bias[0], ssm_a_log[0], ssm_d[0], ssm_norm_w[0])
    xa = moe_layer(g, xa, g.n, mods[0], norm_ffn[0], router_w[0], router_bias[0], exp_w1[0], exp_w3[0], exp_w2[0],
                   sh_w1[0], sh_w3[0], sh_w2[0], final_norm, False)
    xl = mla_layer(g, xa, mods[1], norm_mix[1], mla_w_in[0], mla_q_norm[0], mla_q_up[0], mla_kv_norm[0],
                   mla_kv_up[0], mla_w_out[0])
    xl = moe_layer(g, xl, g.nl, mods[1], norm_ffn[1], router_w[1], router_bias[1], exp_w1[1], exp_w3[1], exp_w2[1],
                   sh_w1[1], sh_w3[1], sh_w2[1], final_norm, True)
    return xl.reshape(bsz, t, d)
```

```python
import functools
import math

import jax
import jax.numpy as jnp
import numpy as np
from jax import lax
from jax.experimental import pallas as pl
from jax.experimental.pallas import tpu as pltpu

F32 = jnp.float32
BF16 = jnp.bfloat16

D_MODEL = 1024
NORM_EPS = 1e-6
GRID_W = 64

RW_HEADS = 8
RW_HEAD = 64
RW_DIM = 512
RW_COLS = 1920
RW_GN_EPS = 64e-5

SSM_HEADS = 16
SSM_HEAD = 64
SSM_DIM = 1024
SSM_GROUPS = 2
SSM_STATE = 128
SSM_CONV = 5
XBC_DIM = 1536
MIX_DIM = 1536

MLA_HEADS = 16
MLA_Q_LORA = 384
MLA_KV_LORA = 256
MLA_NOPE = 64
MLA_ROPE = 32
MLA_V = 64
ROPE_THETA = 10000.0

N_EXPERTS = 64
TOP_K = 6
N_EXPERT_GROUPS = 8
TOPK_GROUPS = 4
EXPERT_FF = 256
ROUTED_SCALE = 2.5

SUBLANES = 8
LANES = 128
HALO = SUBLANES
VMEM_LIMIT = 56 * 2**20


class Geom:
    def __init__(self, b, t, c):
        self.b, self.t, self.c = b, t, c
        self.nl, self.nc = b * t, b * c
        self.n = self.nl + self.nc


def _row_tile(g, want, within_seq=False):
    tile = math.gcd(want, g.t)
    if within_seq:
        tile = math.gcd(tile, g.c)
    assert g.nc % tile == 0 and tile % SUBLANES == 0
    return tile


def _cparams(*sem):
    return pltpu.CompilerParams(dimension_semantics=sem, vmem_limit_bytes=VMEM_LIMIT)


def _full(shape):
    nd = len(shape)
    return pl.BlockSpec(shape, lambda *_: (0,) * nd)


def _bdot(a, b):
    return jnp.dot(a.astype(BF16), b.astype(BF16), preferred_element_type=F32)


def _split3(a):
    a1 = a.astype(BF16)
    r = a - a1.astype(F32)
    a2 = r.astype(BF16)
    r = r - a2.astype(F32)
    return a1, a2, r.astype(BF16)


def _xdot_l(a, m):
    return sum(jnp.dot(p, m, preferred_element_type=F32) for p in _split3(a))


def _xdot_r(m, a):
    return sum(jnp.dot(m, p, preferred_element_type=F32) for p in _split3(a))


def _sigmoid(x):
    return 0.5 * (1.0 + jnp.tanh(0.5 * x))


def _silu(x):
    return x * _sigmoid(x)


def _softplus(x):
    return jnp.maximum(x, 0.0) + jnp.log(1.0 + jnp.exp(-jnp.abs(x)))


def _seq_pos(r, tile, g):
    n_lat = g.nl // tile
    per_l, per_c = g.t // tile, g.c // tile
    is_lat = r < n_lat
    pos = jnp.where(is_lat, r % per_l, (r - n_lat) % per_c)
    per = jnp.where(is_lat, per_l, per_c)
    return pos == 0, pos == per - 1


def _mod_spec(g, tile, col):
    return pl.BlockSpec((1, 1, D_MODEL), lambda r, *_: (jnp.minimum(r * tile // g.t, g.b), 0, col))


def _halo_specs(g, tile, width):
    nb = g.n // HALO
    per = tile // HALO
    prev = pl.BlockSpec((HALO, width), lambda r: (jnp.maximum(r * per - 1, 0), 0))
    nxt = pl.BlockSpec((HALO, width), lambda r: (jnp.minimum((r + 1) * per, nb - 1), 0))
    return prev, nxt


def _fill_halo(scr, cur_ref, prev_ref, next_ref, first, last, tile):
    scr[pl.ds(0, HALO), :] = prev_ref[...] * jnp.where(first, 0.0, 1.0)
    scr[pl.ds(HALO, tile), :] = cur_ref[...]
    scr[pl.ds(HALO + tile, HALO), :] = next_ref[...] * jnp.where(last, 0.0, 1.0)


def _mods_kernel(c_ref, w_ref, b_ref, o_ref):
    o_ref[0] = _bdot(_silu(c_ref[...]), w_ref[0]) + b_ref[0]


def ada_mods(cond, ada_w, ada_b):
    depth, d, n6 = ada_w.shape
    rows = cond.shape[0]
    tn = 1536
    return pl.pallas_call(
        _mods_kernel,
        grid=(depth, n6 // tn),
        in_specs=[pl.BlockSpec((rows, d), lambda l, j: (0, 0)),
                  pl.BlockSpec((1, d, tn), lambda l, j: (l, 0, j)),
                  pl.BlockSpec((1, 1, tn), lambda l, j: (l, 0, j))],
        out_specs=pl.BlockSpec((1, rows, tn), lambda l, j: (l, 0, j)),
        out_shape=jax.ShapeDtypeStruct((depth, rows, n6), F32),
        compiler_params=_cparams("arbitrary", "arbitrary"),
        name="ada_mods",
    )(cond, ada_w, ada_b.reshape(depth, 1, n6))


def _normmod(x, gain, shift, scale):
    ms = jnp.mean(x * x, axis=-1, keepdims=True)
    h = x * lax.rsqrt(ms + NORM_EPS) * gain
    return h * (1.0 + scale) + shift


def _normmod_mm_kernel(x_ref, g_ref, sh_ref, sc_ref, *refs):
    nw = len(refs) // 2
    h = _normmod(x_ref[...], g_ref[...], sh_ref[0], sc_ref[0]).astype(BF16)
    for w_ref, o_ref in zip(refs[:nw], refs[nw:]):
        o_ref[...] = jnp.dot(h, w_ref[...], preferred_element_type=F32).astype(o_ref.dtype)


def normmod_mm(g, xa, nrows, gain, mods, shift_col, scale_col, weights, tile=256):
    d = xa.shape[1]
    tile = _row_tile(g, tile)
    return pl.pallas_call(
        _normmod_mm_kernel,
        grid=(nrows // tile,),
        in_specs=[pl.BlockSpec((tile, d), lambda r: (r, 0)), _full((1, d)),
                  _mod_spec(g, tile, shift_col), _mod_spec(g, tile, scale_col)]
                 + [_full(w.shape) for w in weights],
        out_specs=[pl.BlockSpec((tile, w.shape[1]), lambda r: (r, 0)) for w in weights],
        out_shape=[jax.ShapeDtypeStruct((nrows, w.shape[1]), F32) for w in weights],
        compiler_params=_cparams("arbitrary"),
        name="normmod_mm",
    )(xa, gain.reshape(1, d), mods, mods, *weights)


def _head_ones():
    i = np.arange(RW_DIM) // RW_HEAD
    return jnp.asarray(i[:, None] == i[None, :], BF16)


def _rw_prep_kernel(g, tile, cur_ref, prev_ref, next_ref, mu_ref, w0_ref, w2_ref, a0_ref, a2_ref, g2_ref,
                    kk_ref, ones_ref, r_out, k_out, v_out, kk_out, wf_out, wb_out, af_out, ab_out, g_out, scr):
    first, last = _seq_pos(pl.program_id(0), tile, g)
    _fill_halo(scr, cur_ref, prev_ref, next_ref, first, last, tile)
    x = cur_ref[...]
    nb = 0.5 * (scr[pl.ds(HALO - 1, tile), :] + scr[pl.ds(HALO + 1, tile), :])
    mixed = x + mu_ref[...] * (nb - x)
    r = mixed[:, 0:512]
    k = mixed[:, 512:1024]
    v = mixed[:, 1024:1536]
    xw = mixed[:, 1536:1664]
    xa = mixed[:, 1664:1792]
    xg = mixed[:, 1792:1920]
    w_log = -_softplus(-(w0_ref[...] + _bdot(jnp.tanh(xw), w2_ref[...]))) - 0.5
    decay = jnp.exp(-jnp.exp(w_log))
    wf_out[...] = decay[:, 0:RW_DIM]
    wb_out[...] = decay[:, RW_DIM:]
    a = _sigmoid(a0_ref[...] + _bdot(xa, a2_ref[...]))
    af_out[...] = a[:, 0:RW_DIM]
    ab_out[...] = a[:, RW_DIM:]
    g_out[...] = _bdot(_sigmoid(xg), g2_ref[...])
    kkr = k * kk_ref[...]
    ss = _xdot_l(kkr * kkr, ones_ref[...])
    r_out[...] = r
    k_out[...] = k
    v_out[...] = v
    kk_out[...] = kkr * lax.rsqrt(ss + 1e-12)


def _blockdiag2(w):
    z = jnp.zeros_like(w[0])
    return jnp.concatenate([jnp.concatenate([w[0], z], 1), jnp.concatenate([z, w[1]], 1)], 0)


def _time_major_spec(g, tile):
    n_lat, per_l, per_c = g.nl // tile, g.t // tile, g.c // tile

    def index(r):
        i = r - n_lat
        lat = r < n_lat
        return jnp.where(lat, per_c + r % per_l, i % per_c), jnp.where(lat, r // per_l, i // per_c)

    return pl.BlockSpec((tile, RW_DIM), index)


def rw_prep(g, rw, mu, w0, w2, a0, a2, g2, k_k, tile):
    prev, nxt = _halo_specs(g, tile, RW_COLS)
    n_out = 9
    return pl.pallas_call(
        functools.partial(_rw_prep_kernel, g, tile),
        grid=(g.n // tile,),
        in_specs=[pl.BlockSpec((tile, RW_COLS), lambda r: (r, 0)), prev, nxt, _full((1, RW_COLS)),
                  _full((1, 1024)), _full((128, 1024)),
                  _full((1, 1024)), _full((128, 1024)), _full((128, 512)), _full((1, 512)), _full((512, 512))],
        out_specs=[_time_major_spec(g, tile)] * n_out,
        out_shape=[jax.ShapeDtypeStruct((g.c + g.t, g.b * RW_DIM), F32)] * n_out,
        scratch_shapes=[pltpu.VMEM((tile + 2 * HALO, RW_COLS), F32)],
        compiler_params=_cparams("arbitrary"),
        name="rw_prep",
    )(rw, rw, rw, mu.reshape(1, -1), w0.reshape(1, -1), _blockdiag2(w2).astype(BF16), a0.reshape(1, -1),
      _blockdiag2(a2).astype(BF16), g2.astype(BF16), k_k.reshape(1, -1), _head_ones())


KEYS_PER_ITER = 8


def _rw_scan_kernel(tt, reverse, r_ref, k_ref, v_ref, kk_ref, w_ref, a_ref, ka_ref, y_ref, s_scr, b_scr, kd_scr):
    @pl.when(pl.program_id(1) == 0)
    def _():
        s_scr[...] = jnp.zeros_like(s_scr)

    a = a_ref[...]
    b_scr[...] = kk_ref[...] * a
    kd_scr[...] = k_ref[...] * (1.0 + (a - 1.0) * ka_ref[...])
    nv = RW_HEAD // SUBLANES

    blocks = [pl.ds(j * SUBLANES, SUBLANES) for j in range(nv)]
    zeros = tuple(jnp.zeros((SUBLANES, LANES), F32) for _ in range(nv))

    def step(i, carry):
        t = tt - 1 - i if reverse else i
        vt = [v_ref[t, sl, :] for sl in blocks]

        def read_pass(c, sa):
            sa = list(sa)
            for u in range(KEYS_PER_ITER):
                k = c * KEYS_PER_ITER + u
                kkb = kk_ref[t, pl.ds(k, 1), :]
                for j, sl in enumerate(blocks):
                    sa[j] = sa[j] + s_scr[k, sl, :] * kkb
            return tuple(sa)

        sa = lax.fori_loop(0, RW_HEAD // KEYS_PER_ITER, read_pass, zeros)

        def update_pass(c, y):
            y = list(y)
            for u in range(KEYS_PER_ITER):
                k = c * KEYS_PER_ITER + u
                wb = w_ref[t, pl.ds(k, 1), :]
                bb = b_scr[t, pl.ds(k, 1), :]
                kb = kd_scr[t, pl.ds(k, 1), :]
                rb = r_ref[t, pl.ds(k, 1), :]
                for j, sl in enumerate(blocks):
                    new = s_scr[k, sl, :] * wb + (vt[j] * kb - sa[j] * bb)
                    s_scr[k, sl, :] = new
                    y[j] = y[j] + new * rb
            return tuple(y)

        y = lax.fori_loop(0, RW_HEAD // KEYS_PER_ITER, update_pass, zeros)
        for j, sl in enumerate(blocks):
            y_ref[t, sl, :] = y[j]
        return carry

    lax.fori_loop(0, tt, step, 0)


def rw_scan(reverse, r, k, v, kk, w, a, ka, c_len, tt):
    s_len, _, chains = r.shape
    nb, ncb = s_len // tt, c_len // tt

    def tb(s):
        return jnp.where(s < ncb, ncb - 1 - s, nb - 1 - (s - ncb)) if reverse else s

    block = pl.BlockSpec((tt, RW_HEAD, LANES), lambda c, s: (tb(s), 0, c))
    return pl.pallas_call(
        functools.partial(_rw_scan_kernel, tt, reverse),
        grid=(chains // LANES, nb),
        in_specs=[block] * 6 + [pl.BlockSpec((RW_HEAD, LANES), lambda c, s: (0, c))],
        out_specs=block,
        out_shape=jax.ShapeDtypeStruct((s_len, RW_HEAD, chains), F32),
        scratch_shapes=[pltpu.VMEM((RW_HEAD, RW_HEAD, LANES), F32), pltpu.VMEM((tt, RW_HEAD, LANES), F32),
                        pltpu.VMEM((tt, RW_HEAD, LANES), F32)],
        compiler_params=_cparams("arbitrary", "arbitrary"),
        name="rw_scan",
    )(r, k, v, kk, w, a, ka)


def _to_chains(u):
    s = u.shape[0]
    return u.reshape(s, -1, RW_HEAD).swapaxes(1, 2)


def _from_chains(y):
    return y.swapaxes(1, 2).reshape(y.shape[0], -1)


def _conv_kernel(g, tile, cur_ref, prev_ref, next_ref, w_ref, b_ref, o_ref, scr):
    first, last = _seq_pos(pl.program_id(0), tile, g)
    _fill_halo(scr, cur_ref, prev_ref, next_ref, first, last, tile)
    acc = b_ref[...] + w_ref[pl.ds(0, 1), :] * scr[pl.ds(HALO - 2, tile), :]
    for j in range(1, SSM_CONV):
        acc = acc + w_ref[pl.ds(j, 1), :] * scr[pl.ds(HALO - 2 + j, tile), :]
    o_ref[...] = _silu(acc)


def ssm_conv(g, xbc, conv_w, conv_b, tile):
    prev, nxt = _halo_specs(g, tile, XBC_DIM)
    row = pl.BlockSpec((tile, XBC_DIM), lambda r: (r, 0))
    return pl.pallas_call(
        functools.partial(_conv_kernel, g, tile),
        grid=(g.n // tile,),
        in_specs=[row, prev, nxt, _full((SSM_CONV, XBC_DIM)), _full((1, XBC_DIM))],
        out_specs=row,
        out_shape=jax.ShapeDtypeStruct((g.n, XBC_DIM), F32),
        scratch_shapes=[pltpu.VMEM((tile + 2 * HALO, XBC_DIM), F32)],
        compiler_params=_cparams("arbitrary"),
        name="ssm_conv",
    )(xbc, xbc, xbc, conv_w, conv_b.reshape(1, -1))


def _ssd_dir(rev, xbc, dtr, bias, aneg, expand, tri, st_ref):
    cl = xbc.shape[0]
    xs = xbc[:, 0:SSM_DIM]
    dt = _softplus(dtr + bias)
    da = dt * aneg
    cs = _xdot_r(tri, da)
    dt_x = _xdot_l(dt, expand)
    cs_x = _xdot_l(cs, expand)
    cs_end = cs_x[0:1, :] if rev else cs_x[cl - 1:cl, :]
    xdt = xs * dt_x
    cs_t = cs.T
    row = lax.broadcasted_iota(jnp.int32, (cl, cl), 0)
    col = lax.broadcasted_iota(jnp.int32, (cl, cl), 1)
    keep = (col >= row) if rev else (col <= row)
    lane = lax.broadcasted_iota(jnp.int32, (cl, LANES), 1)
    per_group = SSM_DIM // SSM_GROUPS
    y_parts = []
    for gi in range(SSM_GROUPS):
        bm = xbc[:, SSM_DIM + gi * SSM_STATE:SSM_DIM + (gi + 1) * SSM_STATE]
        cm = xbc[:, SSM_DIM + SSM_GROUPS * SSM_STATE + gi * SSM_STATE:
                 SSM_DIM + SSM_GROUPS * SSM_STATE + (gi + 1) * SSM_STATE]
        cb = lax.dot_general(cm.astype(BF16), bm.astype(BF16), (((1,), (1,)), ((), ())),
                             preferred_element_type=F32)
        gs = slice(gi * per_group, (gi + 1) * per_group)
        s_in = st_ref[:, gs]
        y_off = _bdot(cm, s_in) * jnp.exp(cs_x[:, gs])
        for p in range(per_group // LANES):
            h0 = (gi * per_group + p * LANES) // SSM_HEAD
            x_pair = xdt[:, h0 * SSM_HEAD:h0 * SSM_HEAD + LANES].astype(BF16)
            ys = []
            for hh in (h0, h0 + 1):
                seg = cs[:, hh:hh + 1] - cs_t[hh:hh + 1, :]
                m = cb * jnp.exp(jnp.where(keep, seg, -jnp.inf))
                ys.append(jnp.dot(m.astype(BF16), x_pair, preferred_element_type=F32))
            y_parts.append(jnp.where(lane < SSM_HEAD, ys[0], ys[1]) + y_off[:, p * LANES:(p + 1) * LANES])
        xd = (xdt[:, gs] * jnp.exp(cs_end[:, gs] - cs_x[:, gs])).astype(BF16)
        st_new = lax.dot_general(bm.astype(BF16), xd, (((0,), (0,)), ((), ())), preferred_element_type=F32)
        st_ref[:, gs] = s_in * jnp.exp(cs_end[:, gs]) + st_new
    return jnp.concatenate(y_parts, axis=1)


def _ssd_kernel(xf_ref, dtf_ref, xb_ref, dtb_ref, bias_ref, aneg_ref, exp_ref, tril_ref, triu_ref,
                yf_ref, yb_ref, st_scr):
    @pl.when(pl.program_id(1) == 0)
    def _():
        st_scr[...] = jnp.zeros_like(st_scr)

    yf_ref[...] = _ssd_dir(False, xf_ref[...], dtf_ref[...], bias_ref[0], aneg_ref[0], exp_ref[...],
                           tril_ref[...], st_scr.at[0])
    yb_ref[...] = _ssd_dir(True, xb_ref[...], dtb_ref[...], bias_ref[1], aneg_ref[1], exp_ref[...],
                           triu_ref[...], st_scr.at[1])


def ssd_scan(g, xbc_act, dt_f, dt_b, dt_bias, a_log, cl):
    ncc, nlc = g.c // cl, g.t // cl
    ns = ncc + nlc

    def rows(b, c):
        return jnp.where(c < ncc, g.nl // cl + b * ncc + c, b * nlc + (c - ncc))

    def fwd(b, s):
        return rows(b, s)

    def bwd(b, s):
        return rows(b, jnp.where(s < ncc, ncc - 1 - s, ns - 1 - (s - ncc)))

    pad = lambda u: jnp.pad(u, ((0, 0), (0, LANES - SSM_HEADS))).reshape(2, 1, LANES)
    expand = jnp.asarray(np.arange(LANES)[:, None] == (np.arange(SSM_DIM) // SSM_HEAD)[None, :], BF16)
    idx = np.arange(cl)
    tril = jnp.asarray(idx[:, None] >= idx[None, :], BF16)
    triu = jnp.asarray(idx[:, None] <= idx[None, :], BF16)
    spec = lambda w, f: pl.BlockSpec((cl, w), lambda b, s: (f(b, s), 0))
    return pl.pallas_call(
        _ssd_kernel,
        grid=(g.b, ns),
        in_specs=[spec(XBC_DIM, fwd), spec(LANES, fwd), spec(XBC_DIM, bwd), spec(LANES, bwd),
                  _full((2, 1, LANES)), _full((2, 1, LANES)), _full((LANES, SSM_DIM)), _full((cl, cl)),
                  _full((cl, cl))],
        out_specs=[spec(SSM_DIM, fwd), spec(SSM_DIM, bwd)],
        out_shape=[jax.ShapeDtypeStruct((g.n, SSM_DIM), F32)] * 2,
        scratch_shapes=[pltpu.VMEM((2, SSM_STATE, SSM_DIM), F32)],
        compiler_params=_cparams("arbitrary", "arbitrary"),
        name="ssd_scan",
    )(xbc_act, dt_f, xbc_act, dt_b, pad(dt_bias), pad(-jnp.exp(a_log)), expand, tril, triu)


def _even_out_kernel(x_ref, gate_ref, yf_ref, yb_ref, r_ref, k_ref, v_ref, af_ref, ab_ref, g_ref,
                     ysf_ref, ysb_ref, xs_ref, z_ref, ka_ref, rk_ref, gnw_ref, gnb_ref, ones_ref,
                     dsk_ref, nw_ref, wo_rw_ref, wo_ss_ref, o_ref):
    ones = ones_ref[...]
    y = yf_ref[...] + yb_ref[...]
    mean = _xdot_l(y, ones) * (1.0 / RW_HEAD)
    dev = y - mean
    var = _xdot_l(dev * dev, ones) * (1.0 / RW_HEAD)
    yn = dev * lax.rsqrt(var + RW_GN_EPS) * gnw_ref[...] + gnb_ref[...]
    r, k, ka = r_ref[...], k_ref[...], ka_ref[...]
    ksum = k * (1.0 + (af_ref[...] - 1.0) * ka) + k * (1.0 + (ab_ref[...] - 1.0) * ka)
    bonus = _xdot_l(r * ksum * rk_ref[...], ones) * v_ref[...]
    out_rw = (yn + bonus) * g_ref[...]

    ys = ysf_ref[...] + ysb_ref[...] + xs_ref[...] * dsk_ref[...]
    ys = ys * _silu(z_ref[...])
    half = SSM_DIM // SSM_GROUPS
    parts = []
    for gi in range(SSM_GROUPS):
        yg = ys[:, gi * half:(gi + 1) * half]
        parts.append(yg * lax.rsqrt(jnp.mean(yg * yg, axis=-1, keepdims=True) + NORM_EPS))
    out_ss = jnp.concatenate(parts, axis=1) * nw_ref[...]
    o = _bdot(out_rw, wo_rw_ref[...]) + _bdot(out_ss, wo_ss_ref[...])
    o_ref[...] = x_ref[...] + gate_ref[0] * o


def even_out(g, xa, mods, y_f, y_b, r, k, v, a_f, a_b, gg, ys_f, ys_b, xbc_act, z, ka, rk, gn_w, gn_b,
             d_skip, norm_w, w_out, tile=256):
    tile = _row_tile(g, tile, within_seq=True)
    row = lambda w: pl.BlockSpec((tile, w), lambda i: (i, 0))
    seq = _time_major_spec(g, tile)
    vec = lambda w: _full((1, w))
    return pl.pallas_call(
        _even_out_kernel,
        grid=(g.n // tile,),
        in_specs=[row(D_MODEL), _mod_spec(g, tile, 2)] + [seq] * 8 + [row(1024), row(1024), row(1024), row(1024),
                  vec(512), vec(512), vec(512), vec(512), _full((512, 512)), vec(1024), vec(1024),
                  _full((RW_DIM, D_MODEL)), _full((SSM_DIM, D_MODEL))],
        out_specs=row(D_MODEL),
        out_shape=jax.ShapeDtypeStruct((g.n, D_MODEL), F32),
        compiler_params=_cparams("arbitrary"),
        name="even_out",
    )(xa, mods, y_f, y_b, r, k, v, a_f, a_b, gg, ys_f, ys_b, xbc_act, z, ka.reshape(1, -1), rk.reshape(1, -1),
      gn_w.reshape(1, -1), gn_b.reshape(1, -1), _head_ones(), jnp.repeat(d_skip, SSM_HEAD).reshape(1, -1),
      norm_w.reshape(1, -1), w_out[:RW_DIM].astype(BF16), w_out[RW_DIM:].astype(BF16))


def even_layer(g, xa, mods, norm_g, w_in, w_out, mu, w0, w2, a0, a2, g2, k_k, k_a, r_k, gn_w, gn_b,
               conv_w, conv_b, dt_bias, a_log, d_skip, ssm_norm_w):
    o_z, o_x, o_dt = RW_COLS, RW_COLS + SSM_DIM, RW_COLS + SSM_DIM + XBC_DIM
    padw = lambda w: jnp.pad(w, ((0, 0), (0, LANES - w.shape[1])))
    weights = [w_in[:, :o_z], w_in[:, o_z:o_x], w_in[:, o_x:o_dt], padw(w_in[:, o_dt:o_dt + SSM_HEADS]),
               padw(w_in[:, o_dt + SSM_HEADS:])]
    rw, z, xbc, dt_f, dt_b = normmod_mm(g, xa, g.n, norm_g, mods, 0, 1, [w.astype(BF16) for w in weights])

    seq_tile = _row_tile(g, 256, within_seq=True)
    r, k, v, kk, w_f, w_b, a_f, a_b, gg = rw_prep(g, rw, mu, w0, w2, a0, a2, g2, k_k, seq_tile)
    ka_c = jnp.tile(k_a.reshape(RW_HEADS, RW_HEAD).T, (1, g.b))
    rc, kc, vc, kkc = (_to_chains(u) for u in (r, k, v, kk))
    tt = math.gcd(32, g.c)
    y_f = _from_chains(rw_scan(False, rc, kc, vc, kkc, _to_chains(w_f), _to_chains(a_f), ka_c, g.c, tt))
    y_b = _from_chains(rw_scan(True, rc, kc, vc, kkc, _to_chains(w_b), _to_chains(a_b), ka_c, g.c, tt))

    xbc_act = ssm_conv(g, xbc, conv_w, conv_b, seq_tile)
    ys_f, ys_b = ssd_scan(g, xbc_act, dt_f, dt_b, dt_bias, a_log, math.gcd(128, g.c))
    return even_out(g, xa, mods, y_f, y_b, r, k, v, a_f, a_b, gg, ys_f, ys_b, xbc_act, z, k_a, r_k, gn_w, gn_b,
                    d_skip, ssm_norm_w, w_out)


def _router_kernel(x_ref, g_ref, sh_ref, sc_ref, rw_ref, rb_ref, h_ref, gates_ref):
    h = _normmod(x_ref[...], g_ref[...], sh_ref[0], sc_ref[0])
    h_ref[...] = h.astype(BF16)
    nt = (((1,), (1,)), ((), ()))
    h3, w3 = _split3(h), _split3(rw_ref[...])
    logits = sum(lax.dot_general(w3[i], h3[j], nt, preferred_element_type=F32)
                 for i in range(3) for j in range(3 - i))
    scores = _sigmoid(logits)
    sel = scores + rb_ref[...]
    tile = sel.shape[1]
    per = N_EXPERTS // N_EXPERT_GROUPS
    grp = sel.reshape(N_EXPERT_GROUPS, per, tile)
    pos = lax.broadcasted_iota(jnp.int32, grp.shape, 1).astype(F32)
    m1 = jnp.max(grp, axis=1, keepdims=True)
    first = jnp.min(jnp.where(grp == m1, pos, float(per)), axis=1, keepdims=True)
    m2 = jnp.max(jnp.where(pos == first, -jnp.inf, grp), axis=1, keepdims=True)
    gsc = jnp.broadcast_to(m1 + m2, grp.shape)
    gid = lax.broadcasted_iota(jnp.int32, grp.shape, 0)
    beaten = jnp.zeros(grp.shape, F32)
    for o in range(N_EXPERT_GROUPS):
        other = gsc[o:o + 1]
        beaten = beaten + jnp.where((other > gsc) | ((other == gsc) & (o < gid)), 1.0, 0.0)
    cand = jnp.where(beaten < TOPK_GROUPS, grp, -jnp.inf).reshape(N_EXPERTS, tile)
    eid = lax.broadcasted_iota(jnp.int32, cand.shape, 0).astype(F32)
    chosen = jnp.zeros(cand.shape, F32)
    for _ in range(TOP_K):
        best = jnp.max(cand, axis=0, keepdims=True)
        pick = eid == jnp.min(jnp.where(cand == best, eid, float(N_EXPERTS)), axis=0, keepdims=True)
        chosen = jnp.where(pick, 1.0, chosen)
        cand = jnp.where(pick, -jnp.inf, cand)
    wts = scores * chosen
    gates = wts / jnp.sum(wts, axis=0, keepdims=True) * ROUTED_SCALE
    gates_ref[...] = jnp.concatenate([gates, jnp.zeros_like(gates)], axis=0).T


def moe_router(g, xa, nrows, gain, mods, router_w, router_bias, tile=512):
    d = xa.shape[1]
    tile = _row_tile(g, tile)
    return pl.pallas_call(
        _router_kernel,
        grid=(nrows // tile,),
        in_specs=[pl.BlockSpec((tile, d), lambda r: (r, 0)), _full((1, d)), _mod_spec(g, tile, 3),
                  _mod_spec(g, tile, 4), _full((N_EXPERTS, d)), _full((N_EXPERTS, 1))],
        out_specs=[pl.BlockSpec((tile, d), lambda r: (r, 0)), pl.BlockSpec((tile, LANES), lambda r: (r, 0))],
        out_shape=[jax.ShapeDtypeStruct((nrows, d), BF16), jax.ShapeDtypeStruct((nrows, LANES), F32)],
        compiler_params=_cparams("arbitrary"),
        name="moe_router",
    )(xa, gain.reshape(1, d), mods, mods, router_w.T, router_bias.reshape(N_EXPERTS, 1))


EXPERTS_PER_STEP = 4


def _experts_kernel(final, h_ref, gates_ref, w1_ref, w3_ref, w2_ref, sw1_ref, sw3_ref, sw2_ref, x_ref, gate_ref,
                    fin_ref, o_ref, acc):
    e = pl.program_id(1)
    h = h_ref[...]
    ff = EXPERT_FF

    def ffn(w1, w3):
        return _silu(jnp.dot(h, w1, preferred_element_type=F32)) * jnp.dot(h, w3, preferred_element_type=F32)

    @pl.when(e == 0)
    def _():
        acc[...] = jnp.dot(ffn(sw1_ref[...], sw3_ref[...]).astype(BF16), sw2_ref[...],
                           preferred_element_type=F32)

    first = e * EXPERTS_PER_STEP
    mine = pltpu.roll(gates_ref[...], jnp.where(first == 0, 0, LANES - first), axis=1)
    hid = jnp.concatenate([ffn(w1_ref[j], w3_ref[j]) * mine[:, j:j + 1] for j in range(EXPERTS_PER_STEP)], axis=1)
    acc[...] += jnp.dot(hid.astype(BF16), w2_ref[0], preferred_element_type=F32)

    @pl.when(e == pl.num_programs(1) - 1)
    def _():
        out = x_ref[...] + gate_ref[0] * acc[...]
        if final:
            ms = jnp.mean(out * out, axis=-1, keepdims=True)
            out = out * lax.rsqrt(ms + NORM_EPS) * fin_ref[...]
        o_ref[...] = out


def moe_experts(g, xa, nrows, mods, h, gates, w1, w3, w2, sw1, sw3, sw2, final_gain, final, tile=1024):
    d = xa.shape[1]
    tile = _row_tile(g, tile)
    eps = EXPERTS_PER_STEP
    w2c = w2.astype(BF16).reshape(N_EXPERTS // eps, eps * EXPERT_FF, d)
    up = pl.BlockSpec((eps, d, EXPERT_FF), lambda i, e: (e, 0, 0))
    return pl.pallas_call(
        functools.partial(_experts_kernel, final),
        grid=(nrows // tile, N_EXPERTS // eps),
        in_specs=[pl.BlockSpec((tile, d), lambda i, e: (i, 0)),
                  pl.BlockSpec((tile, LANES), lambda i, e: (i, 0)),
                  up, up, pl.BlockSpec((1, eps * EXPERT_FF, d), lambda i, e: (e, 0, 0)),
                  _full(sw1.shape), _full(sw3.shape), _full(sw2.shape),
                  pl.BlockSpec((tile, d), lambda i, e: (i, 0)), _mod_spec(g, tile, 5), _full((1, d))],
        out_specs=pl.BlockSpec((tile, d), lambda i, e: (i, 0)),
        out_shape=jax.ShapeDtypeStruct((nrows, d), F32),
        scratch_shapes=[pltpu.VMEM((tile, d), F32)],
        compiler_params=_cparams("arbitrary", "arbitrary"),
        name="moe_experts",
    )(h, gates, w1.astype(BF16), w3.astype(BF16), w2c, sw1.astype(BF16), sw3.astype(BF16), sw2.astype(BF16),
      xa, mods, final_gain.reshape(1, d))


def moe_layer(g, xa, nrows, mods, gain, router_w, router_bias, w1, w3, w2, sw1, sw3, sw2, final_gain, final):
    h, gates = moe_router(g, xa, nrows, gain, mods, router_w, router_bias)
    return moe_experts(g, xa, nrows, mods, h, gates, w1, w3, w2, sw1, sw3, sw2, final_gain, final)


HEAD_PAD = 128


def _rope_tables(g, scale):
    rows = g.t // GRID_W
    r_idx = jnp.repeat(jnp.arange(rows, dtype=F32), GRID_W)
    c_idx = jnp.tile(jnp.arange(GRID_W, dtype=F32), rows)
    axis_dim = MLA_ROPE // 2
    inv_freq = ROPE_THETA ** (-jnp.arange(0, axis_dim, 2, dtype=F32) / axis_dim)
    ang = jnp.concatenate([r_idx[:, None] * inv_freq, c_idx[:, None] * inv_freq], axis=-1)
    cos, sin = jnp.cos(ang), jnp.sin(ang)
    cos2 = jnp.concatenate([cos, cos], axis=1)
    sin2 = jnp.concatenate([-sin, sin], axis=1)
    t = g.t
    q_cos = jnp.concatenate([jnp.ones((t, MLA_NOPE), F32), cos2, jnp.zeros((t, 32), F32)], axis=1) * scale
    q_sin = jnp.concatenate([jnp.zeros((t, MLA_NOPE), F32), sin2, jnp.zeros((t, 32), F32)], axis=1) * scale
    zpad = jnp.zeros((t, LANES - MLA_ROPE), F32)
    k_cos = jnp.concatenate([cos2, zpad], axis=1)
    k_sin = jnp.concatenate([sin2, zpad], axis=1)
    c_cos = jnp.concatenate([jnp.ones((g.c, MLA_ROPE), F32), jnp.zeros((g.c, LANES - MLA_ROPE), F32)], axis=1)
    k_cos = jnp.concatenate([k_cos, c_cos], axis=0)
    k_sin = jnp.concatenate([k_sin, jnp.zeros((g.c, LANES), F32)], axis=0)
    return q_cos, q_sin, k_cos, k_sin


def _q_kernel(cq_ref, qn_ref, w1_ref, w2_ref, cos_ref, sin_ref, q_ref):
    cq = cq_ref[...]
    qn = (cq * lax.rsqrt(jnp.mean(cq * cq, axis=-1, keepdims=True) + NORM_EPS) * qn_ref[...]).astype(BF16)
    q1 = jnp.dot(qn, w1_ref[...], preferred_element_type=F32)
    q2 = jnp.dot(qn, w2_ref[...], preferred_element_type=F32)
    cos, sin = cos_ref[...], sin_ref[...]
    for h in range(MLA_HEADS):
        sl = slice(h * HEAD_PAD, (h + 1) * HEAD_PAD)
        q_ref[:, sl] = (q1[:, sl] * cos + q2[:, sl] * sin).astype(BF16)


def mla_queries(g, cq, q_norm, w1, w2, q_cos, q_sin, tile):
    per = g.t // tile
    width = MLA_HEADS * HEAD_PAD
    return pl.pallas_call(
        _q_kernel,
        grid=(g.nl // tile,),
        in_specs=[pl.BlockSpec((tile, MLA_Q_LORA), lambda r: (r, 0)), _full((1, MLA_Q_LORA)),
                  _full(w1.shape), _full(w2.shape),
                  pl.BlockSpec((tile, HEAD_PAD), lambda r: (r % per, 0)),
                  pl.BlockSpec((tile, HEAD_PAD), lambda r: (r % per, 0))],
        out_specs=pl.BlockSpec((tile, width), lambda r: (r, 0)),
        out_shape=jax.ShapeDtypeStruct((g.nl, width), BF16),
        compiler_params=_cparams("arbitrary"),
        name="mla_queries",
    )(cq, q_norm.reshape(1, -1), w1, w2, q_cos, q_sin)


def _kv_kernel(ckv_ref, kr1_ref, kr2_ref, kvn_ref, wk_ref, wv_ref, place_ref, cos_ref, sin_ref, k_ref, v_ref):
    ckv = ckv_ref[...]
    kvn = (ckv * lax.rsqrt(jnp.mean(ckv * ckv, axis=-1, keepdims=True) + NORM_EPS) * kvn_ref[...]).astype(BF16)
    kr = (kr1_ref[...] * cos_ref[...] + kr2_ref[...] * sin_ref[...]).astype(BF16)
    k = jnp.dot(kvn, wk_ref[...], preferred_element_type=F32) + jnp.dot(kr, place_ref[...], preferred_element_type=F32)
    k_ref[...] = k.astype(BF16)
    v_ref[...] = jnp.dot(kvn, wv_ref[...], preferred_element_type=F32).astype(BF16)


def mla_keys_values(g, ckv, kr1, kr2, kv_norm, wk, wv, place, k_cos, k_sin, tile):
    n_lat, per_l, per_c = g.nl // tile, g.t // tile, g.c // tile

    def tab(r):
        return (jnp.where(r < n_lat, r % per_l, per_l + (r - n_lat) % per_c), 0)

    row = lambda w: pl.BlockSpec((tile, w), lambda r: (r, 0))
    kw = MLA_HEADS * HEAD_PAD
    return pl.pallas_call(
        _kv_kernel,
        grid=(g.n // tile,),
        in_specs=[row(MLA_KV_LORA), row(LANES), row(LANES), _full((1, MLA_KV_LORA)), _full(wk.shape),
                  _full(wv.shape), _full(place.shape), pl.BlockSpec((tile, LANES), tab),
                  pl.BlockSpec((tile, LANES), tab)],
        out_specs=[row(kw), row(MLA_HEADS * MLA_V)],
        out_shape=[jax.ShapeDtypeStruct((g.n, kw), BF16), jax.ShapeDtypeStruct((g.n, MLA_HEADS * MLA_V), BF16)],
        compiler_params=_cparams("arbitrary"),
        name="mla_keys_values",
    )(ckv, kr1, kr2, kv_norm.reshape(1, -1), wk, wv, place, k_cos, k_sin)


def _attn_kernel(q_ref, kl_ref, kc_ref, vl_ref, vc_ref, o_ref):
    nt = (((1,), (1,)), ((), ()))
    vl, vc = vl_ref[...], vc_ref[...]
    outs = []
    for hh in range(2):
        sl = slice(hh * HEAD_PAD, (hh + 1) * HEAD_PAD)
        q = q_ref[:, sl]
        s_l = lax.dot_general(q, kl_ref[:, sl], nt, preferred_element_type=F32)
        s_c = lax.dot_general(q, kc_ref[:, sl], nt, preferred_element_type=F32)
        m = jnp.maximum(jnp.max(s_l, axis=-1, keepdims=True), jnp.max(s_c, axis=-1, keepdims=True))
        p_l, p_c = jnp.exp(s_l - m), jnp.exp(s_c - m)
        den = jnp.sum(p_l, axis=-1, keepdims=True) + jnp.sum(p_c, axis=-1, keepdims=True)
        o = (jnp.dot(p_l.astype(BF16), vl, preferred_element_type=F32)
             + jnp.dot(p_c.astype(BF16), vc, preferred_element_type=F32))
        outs.append(o / den)
    lane = lax.broadcasted_iota(jnp.int32, outs[0].shape, 1)
    o_ref[...] = jnp.where(lane < MLA_V, outs[0], outs[1]).astype(o_ref.dtype)


def mla_attention(g, q, k, v, tq):
    nq = g.t // tq
    pairs = MLA_HEADS // 2
    pw = 2 * HEAD_PAD
    ctx0 = g.nl // g.c
    return pl.pallas_call(
        _attn_kernel,
        grid=(g.b, pairs, nq),
        in_specs=[pl.BlockSpec((tq, pw), lambda b, p, i: (b * nq + i, p)),
                  pl.BlockSpec((g.t, pw), lambda b, p, i: (b, p)),
                  pl.BlockSpec((g.c, pw), lambda b, p, i: (ctx0 + b, p)),
                  pl.BlockSpec((g.t, 2 * MLA_V), lambda b, p, i: (b, p)),
                  pl.BlockSpec((g.c, 2 * MLA_V), lambda b, p, i: (ctx0 + b, p))],
        out_specs=pl.BlockSpec((tq, 2 * MLA_V), lambda b, p, i: (b * nq + i, p)),
        out_shape=jax.ShapeDtypeStruct((g.nl, MLA_HEADS * MLA_V), BF16),
        compiler_params=_cparams("arbitrary", "arbitrary", "arbitrary"),
        name="mla_attention",
    )(q, k, k, v, v)


def _proj_resid_kernel(x_ref, gate_ref, a_ref, w_ref, o_ref):
    o_ref[...] = x_ref[...] + gate_ref[0] * jnp.dot(a_ref[...], w_ref[...], preferred_element_type=F32)


def proj_resid(g, xa, nrows, mods, act, w, tile=512):
    d = xa.shape[1]
    tile = _row_tile(g, tile)
    return pl.pallas_call(
        _proj_resid_kernel,
        grid=(nrows // tile,),
        in_specs=[pl.BlockSpec((tile, d), lambda r: (r, 0)), _mod_spec(g, tile, 2),
                  pl.BlockSpec((tile, act.shape[1]), lambda r: (r, 0)), _full(w.shape)],
        out_specs=pl.BlockSpec((tile, d), lambda r: (r, 0)),
        out_shape=jax.ShapeDtypeStruct((nrows, d), F32),
        compiler_params=_cparams("arbitrary"),
        name="proj_resid",
    )(xa, mods, act, w)


def mla_layer(g, xa, mods, norm_g, w_in, q_norm, q_up, kv_norm, kv_up, w_out):
    scale = (MLA_NOPE + MLA_ROPE) ** -0.5
    half = MLA_ROPE // 2
    o_kv, o_kr = MLA_Q_LORA, MLA_Q_LORA + MLA_KV_LORA
    w_kr = w_in[:, o_kr:]
    padw = lambda w: jnp.pad(w, ((0, 0), (0, LANES - w.shape[1])))
    weights = [w_in[:, :o_kv], w_in[:, o_kv:o_kr], padw(w_kr),
               padw(jnp.concatenate([w_kr[:, half:], w_kr[:, :half]], axis=1))]
    cq, ckv, kr1, kr2 = normmod_mm(g, xa, g.n, norm_g, mods, 0, 1, [w.astype(BF16) for w in weights])

    qu = q_up.reshape(MLA_Q_LORA, MLA_HEADS, MLA_NOPE + MLA_ROPE)
    zq = jnp.zeros((MLA_Q_LORA, MLA_HEADS, HEAD_PAD - MLA_NOPE - MLA_ROPE), F32)
    w1 = jnp.concatenate([qu, zq], axis=2).reshape(MLA_Q_LORA, -1).astype(BF16)
    swapped = jnp.concatenate([qu[:, :, MLA_NOPE + half:], qu[:, :, MLA_NOPE:MLA_NOPE + half]], axis=2)
    w2 = jnp.concatenate([jnp.zeros((MLA_Q_LORA, MLA_HEADS, MLA_NOPE), F32), swapped, zq], axis=2)
    w2 = w2.reshape(MLA_Q_LORA, -1).astype(BF16)
    kvu = kv_up.reshape(MLA_KV_LORA, MLA_HEADS, MLA_NOPE + MLA_V)
    wk = jnp.concatenate([kvu[:, :, :MLA_NOPE], jnp.zeros((MLA_KV_LORA, MLA_HEADS, HEAD_PAD - MLA_NOPE), F32)],
                         axis=2).reshape(MLA_KV_LORA, -1).astype(BF16)
    wv = kvu[:, :, MLA_NOPE:].reshape(MLA_KV_LORA, -1).astype(BF16)
    col = np.arange(MLA_HEADS * HEAD_PAD)
    place = jnp.asarray((col[None, :] % HEAD_PAD) == (np.arange(LANES)[:, None] + MLA_NOPE), BF16)
    place = place * jnp.asarray(np.arange(LANES)[:, None] < MLA_ROPE, BF16)

    q_cos, q_sin, k_cos, k_sin = _rope_tables(g, scale)
    tile = _row_tile(g, 256, within_seq=True)
    q = mla_queries(g, cq, q_norm, w1, w2, q_cos, q_sin, tile)
    k, v = mla_keys_values(g, ckv, kr1, kr2, kv_norm, wk, wv, place, k_cos, k_sin, tile)
    attn = mla_attention(g, q, k, v, math.gcd(256, g.t))
    return proj_resid(g, xa, g.nl, mods, attn, w_out.astype(BF16))


def kernel(x, c, ctx, c_ctx, ada_w, ada_b, norm_mix, norm_ffn, ev_w_in, ev_w_out, rw_mu, rw_w0, rw_w2, rw_a0, rw_a2, rw_g2, rw_kk, rw_ka, rw_rk, rw_gn_w, rw_gn_b, ssm_conv_w, ssm_conv_b, ssm_dt_bias, ssm_a_log, ssm_d, ssm_norm_w, mla_w_in, mla_q_norm, mla_q_up, mla_kv_norm, mla_kv_up, mla_w_out, router_w, router_bias, exp_w1, exp_w3, exp_w2, sh_w1, sh_w3, sh_w2, final_norm):
    bsz, t, d = x.shape
    g = Geom(bsz, t, ctx.shape[1])
    depth = ada_w.shape[0]
    assert depth == 2 and d == D_MODEL, "layer 0 is the recurrent mixer, layer 1 the final attention layer"

    rows = -(-(bsz + 1) // SUBLANES) * SUBLANES
    cond = jnp.concatenate([c, c_ctx[None, :], jnp.zeros((rows - bsz - 1, d), F32)], axis=0)
    mods = ada_mods(cond, ada_w, ada_b).reshape(depth, rows, 1, 6 * d)
    xa = jnp.concatenate([x.reshape(g.nl, d), ctx.reshape(g.nc, d)], axis=0)

    xa = even_layer(g, xa, mods[0], norm_mix[0], ev_w_in[0], ev_w_out[0], rw_mu[0], rw_w0[0], rw_w2[0], rw_a0[0],
                    rw_a2[0], rw_g2[0], rw_kk[0], rw_ka[0], rw_rk[0], rw_gn_w[0], rw_gn_b[0], ssm_conv_w[0],
                    ssm_conv_b[0], ssm_dt_bias[0], ssm_a_log[0], ssm_d[0], ssm_norm_w[0])
    xa = moe_layer(g, xa, g.n, mods[0], norm_ffn[0], router_w[0], router_bias[0], exp_w1[0], exp_w3[0], exp_w2[0],
                   sh_w1[0], sh_w3[0], sh_w2[0], final_norm, False)
    xl = mla_layer(g, xa, mods[1], norm_mix[1], mla_w_in[0], mla_q_norm[0], mla_q_up[0], mla_kv_norm[0],
                   mla_kv_up[0], mla_w_out[0])
    xl = moe_layer(g, xl, g.nl, mods[1], norm_ffn[1], router_w[1], router_bias[1], exp_w1[1], exp_w3[1], exp_w2[1],
                   sh_w1[1], sh_w3[1], sh_w2[1], final_norm, True)
    return xl.reshape(bsz, t, d)
```

```python
import functools
import math

import jax
import jax.numpy as jnp
import numpy as np
from jax import lax
from jax.experimental import pallas as pl
from jax.experimental.pallas import tpu as pltpu

F32 = jnp.float32
BF16 = jnp.bfloat16

D_MODEL = 1024
NORM_EPS = 1e-6
GRID_W = 64

RW_HEADS = 8
RW_HEAD = 64
RW_DIM = 512
RW_COLS = 1920
RW_GN_EPS = 64e-5

SSM_HEADS = 16
SSM_HEAD = 64
SSM_DIM = 1024
SSM_GROUPS = 2
SSM_STATE = 128
SSM_CONV = 5
XBC_DIM = 1536
MIX_DIM = 1536

MLA_HEADS = 16
MLA_Q_LORA = 384
MLA_KV_LORA = 256
MLA_NOPE = 64
MLA_ROPE = 32
MLA_V = 64
ROPE_THETA = 10000.0

N_EXPERTS = 64
TOP_K = 6
N_EXPERT_GROUPS = 8
TOPK_GROUPS = 4
EXPERT_FF = 256
ROUTED_SCALE = 2.5

SUBLANES = 8
LANES = 128
HALO = SUBLANES
VMEM_LIMIT = 56 * 2**20


class Geom:
    def __init__(self, b, t, c):
        self.b, self.t, self.c = b, t, c
        self.nl, self.nc = b * t, b * c
        self.n = self.nl + self.nc


def _row_tile(g, want, within_seq=False):
    tile = math.gcd(want, g.t)
    if within_seq:
        tile = math.gcd(tile, g.c)
    assert g.nc % tile == 0 and tile % SUBLANES == 0
    return tile


def _cparams(*sem):
    return pltpu.CompilerParams(dimension_semantics=sem, vmem_limit_bytes=VMEM_LIMIT)


def _full(shape):
    nd = len(shape)
    return pl.BlockSpec(shape, lambda *_: (0,) * nd)


def _bdot(a, b):
    return jnp.dot(a.astype(BF16), b.astype(BF16), preferred_element_type=F32)


def _split3(a):
    a1 = a.astype(BF16)
    r = a - a1.astype(F32)
    a2 = r.astype(BF16)
    r = r - a2.astype(F32)
    return a1, a2, r.astype(BF16)


def _xdot_l(a, m):
    return sum(jnp.dot(p, m, preferred_element_type=F32) for p in _split3(a))


def _xdot_r(m, a):
    return sum(jnp.dot(m, p, preferred_element_type=F32) for p in _split3(a))


def _sigmoid(x):
    return 0.5 * (1.0 + jnp.tanh(0.5 * x))


def _silu(x):
    return x * _sigmoid(x)


def _softplus(x):
    return jnp.maximum(x, 0.0) + jnp.log(1.0 + jnp.exp(-jnp.abs(x)))


def _seq_pos(r, tile, g):
    n_lat = g.nl // tile
    per_l, per_c = g.t // tile, g.c // tile
    is_lat = r < n_lat
    pos = jnp.where(is_lat, r % per_l, (r - n_lat) % per_c)
    per = jnp.where(is_lat, per_l, per_c)
    return pos == 0, pos == per - 1


def _mod_spec(g, tile, col):
    return pl.BlockSpec((1, 1, D_MODEL), lambda r, *_: (jnp.minimum(r * tile // g.t, g.b), 0, col))


def _halo_specs(g, tile, width):
    nb = g.n // HALO
    per = tile // HALO
    prev = pl.BlockSpec((HALO, width), lambda r: (jnp.maximum(r * per - 1, 0), 0))
    nxt = pl.BlockSpec((HALO, width), lambda r: (jnp.minimum((r + 1) * per, nb - 1), 0))
    return prev, nxt


def _fill_halo(scr, cur_ref, prev_ref, next_ref, first, last, tile):
    scr[pl.ds(0, HALO), :] = prev_ref[...] * jnp.where(first, 0.0, 1.0)
    scr[pl.ds(HALO, tile), :] = cur_ref[...]
    scr[pl.ds(HALO + tile, HALO), :] = next_ref[...] * jnp.where(last, 0.0, 1.0)


def _mods_kernel(c_ref, w_ref, b_ref, o_ref):
    o_ref[0] = _bdot(_silu(c_ref[...]), w_ref[0]) + b_ref[0]


def ada_mods(cond, ada_w, ada_b):
    depth, d, n6 = ada_w.shape
    rows = cond.shape[0]
    tn = 1536
    return pl.pallas_call(
        _mods_kernel,
        grid=(depth, n6 // tn),
        in_specs=[pl.BlockSpec((rows, d), lambda l, j: (0, 0)),
                  pl.BlockSpec((1, d, tn), lambda l, j: (l, 0, j)),
                  pl.BlockSpec((1, 1, tn), lambda l, j: (l, 0, j))],
        out_specs=pl.BlockSpec((1, rows, tn), lambda l, j: (l, 0, j)),
        out_shape=jax.ShapeDtypeStruct((depth, rows, n6), F32),
        compiler_params=_cparams("arbitrary", "arbitrary"),
        name="ada_mods",
    )(cond, ada_w, ada_b.reshape(depth, 1, n6))


def _normmod(x, gain, shift, scale):
    ms = jnp.mean(x * x, axis=-1, keepdims=True)
    h = x * lax.rsqrt(ms + NORM_EPS) * gain
    return h * (1.0 + scale) + shift


def _normmod_mm_kernel(x_ref, g_ref, sh_ref, sc_ref, *refs):
    nw = len(refs) // 2
    h = _normmod(x_ref[...], g_ref[...], sh_ref[0], sc_ref[0]).astype(BF16)
    for w_ref, o_ref in zip(refs[:nw], refs[nw:]):
        o_ref[...] = jnp.dot(h, w_ref[...], preferred_element_type=F32).astype(o_ref.dtype)


def normmod_mm(g, xa, nrows, gain, mods, shift_col, scale_col, weights, tile=256):
    d = xa.shape[1]
    tile = _row_tile(g, tile)
    return pl.pallas_call(
        _normmod_mm_kernel,
        grid=(nrows // tile,),
        in_specs=[pl.BlockSpec((tile, d), lambda r: (r, 0)), _full((1, d)),
                  _mod_spec(g, tile, shift_col), _mod_spec(g, tile, scale_col)]
                 + [_full(w.shape) for w in weights],
        out_specs=[pl.BlockSpec((tile, w.shape[1]), lambda r: (r, 0)) for w in weights],
        out_shape=[jax.ShapeDtypeStruct((nrows, w.shape[1]), F32) for w in weights],
        compiler_params=_cparams("arbitrary"),
        name="normmod_mm",
    )(xa, gain.reshape(1, d), mods, mods, *weights)


def _head_ones():
    i = np.arange(RW_DIM) // RW_HEAD
    return jnp.asarray(i[:, None] == i[None, :], BF16)


def _rw_prep_kernel(g, tile, cur_ref, prev_ref, next_ref, mu_ref, w0_ref, w2_ref, a0_ref, a2_ref, g2_ref,
                    kk_ref, ones_ref, r_out, k_out, v_out, kk_out, wf_out, wb_out, af_out, ab_out, g_out, scr):
    first, last = _seq_pos(pl.program_id(0), tile, g)
    _fill_halo(scr, cur_ref, prev_ref, next_ref, first, last, tile)
    x = cur_ref[...]
    nb = 0.5 * (scr[pl.ds(HALO - 1, tile), :] + scr[pl.ds(HALO + 1, tile), :])
    mixed = x + mu_ref[...] * (nb - x)
    r = mixed[:, 0:512]
    k = mixed[:, 512:1024]
    v = mixed[:, 1024:1536]
    xw = mixed[:, 1536:1664]
    xa = mixed[:, 1664:1792]
    xg = mixed[:, 1792:1920]
    w_log = -_softplus(-(w0_ref[...] + _bdot(jnp.tanh(xw), w2_ref[...]))) - 0.5
    decay = jnp.exp(-jnp.exp(w_log))
    wf_out[...] = decay[:, 0:RW_DIM]
    wb_out[...] = decay[:, RW_DIM:]
    a = _sigmoid(a0_ref[...] + _bdot(xa, a2_ref[...]))
    af_out[...] = a[:, 0:RW_DIM]
    ab_out[...] = a[:, RW_DIM:]
    g_out[...] = _bdot(_sigmoid(xg), g2_ref[...])
    kkr = k * kk_ref[...]
    ss = _xdot_l(kkr * kkr, ones_ref[...])
    r_out[...] = r
    k_out[...] = k
    v_out[...] = v
    kk_out[...] = kkr * lax.rsqrt(ss + 1e-12)


def _blockdiag2(w):
    z = jnp.zeros_like(w[0])
    return jnp.concatenate([jnp.concatenate([w[0], z], 1), jnp.concatenate([z, w[1]], 1)], 0)


def _time_major_spec(g, tile):
    n_lat, per_l, per_c = g.nl // tile, g.t // tile, g.c // tile

    def index(r):
        i = r - n_lat
        lat = r < n_lat
        return jnp.where(lat, per_c + r % per_l, i % per_c), jnp.where(lat, r // per_l, i // per_c)

    return pl.BlockSpec((tile, RW_DIM), index)


def rw_prep(g, rw, mu, w0, w2, a0, a2, g2, k_k, tile):
    prev, nxt = _halo_specs(g, tile, RW_COLS)
    n_out = 9
    return pl.pallas_call(
        functools.partial(_rw_prep_kernel, g, tile),
        grid=(g.n // tile,),
        in_specs=[pl.BlockSpec((tile, RW_COLS), lambda r: (r, 0)), prev, nxt, _full((1, RW_COLS)),
                  _full((1, 1024)), _full((128, 1024)),
                  _full((1, 1024)), _full((128, 1024)), _full((128, 512)), _full((1, 512)), _full((512, 512))],
        out_specs=[_time_major_spec(g, tile)] * n_out,
        out_shape=[jax.ShapeDtypeStruct((g.c + g.t, g.b * RW_DIM), F32)] * n_out,
        scratch_shapes=[pltpu.VMEM((tile + 2 * HALO, RW_COLS), F32)],
        compiler_params=_cparams("arbitrary"),
        name="rw_prep",
    )(rw, rw, rw, mu.reshape(1, -1), w0.reshape(1, -1), _blockdiag2(w2).astype(BF16), a0.reshape(1, -1),
      _blockdiag2(a2).astype(BF16), g2.astype(BF16), k_k.reshape(1, -1), _head_ones())


KEYS_PER_ITER = 8


def _rw_scan_kernel(tt, reverse, r_ref, k_ref, v_ref, kk_ref, w_ref, a_ref, ka_ref, y_ref, s_scr, b_scr, kd_scr):
    @pl.when(pl.program_id(1) == 0)
    def _():
        s_scr[...] = jnp.zeros_like(s_scr)

    a = a_ref[...]
    b_scr[...] = kk_ref[...] * a
    kd_scr[...] = k_ref[...] * (1.0 + (a - 1.0) * ka_ref[...])
    nv = RW_HEAD // SUBLANES

    blocks = [pl.ds(j * SUBLANES, SUBLANES) for j in range(nv)]
    zeros = tuple(jnp.zeros((SUBLANES, LANES), F32) for _ in range(nv))

    def step(i, carry):
        t = tt - 1 - i if reverse else i
        vt = [v_ref[t, sl, :] for sl in blocks]

        def read_pass(c, sa):
            sa = list(sa)
            for u in range(KEYS_PER_ITER):
                k = c * KEYS_PER_ITER + u
                kkb = kk_ref[t, pl.ds(k, 1), :]
                for j, sl in enumerate(blocks):
                    sa[j] = sa[j] + s_scr[k, sl, :] * kkb
            return tuple(sa)

        sa = lax.fori_loop(0, RW_HEAD // KEYS_PER_ITER, read_pass, zeros)

        def update_pass(c, y):
            y = list(y)
            for u in range(KEYS_PER_ITER):
                k = c * KEYS_PER_ITER + u
                wb = w_ref[t, pl.ds(k, 1), :]
                bb = b_scr[t, pl.ds(k, 1), :]
                kb = kd_scr[t, pl.ds(k, 1), :]
                rb = r_ref[t, pl.ds(k, 1), :]
                for j, sl in enumerate(blocks):
                    new = s_scr[k, sl, :] * wb + (vt[j] * kb - sa[j] * bb)
                    s_scr[k, sl, :] = new
                    y[j] = y[j] + new * rb
            return tuple(y)

        y = lax.fori_loop(0, RW_HEAD // KEYS_PER_ITER, update_pass, zeros)
        for j, sl in enumerate(blocks):
            y_ref[t, sl, :] = y[j]
        return carry

    lax.fori_loop(0, tt, step, 0)


def rw_scan(reverse, r, k, v, kk, w, a, ka, c_len, tt):
    s_len, _, chains = r.shape
    nb, ncb = s_len // tt, c_len // tt

    def tb(s):
        return jnp.where(s < ncb, ncb - 1 - s, nb - 1 - (s - ncb)) if reverse else s

    block = pl.BlockSpec((tt, RW_HEAD, LANES), lambda c, s: (tb(s), 0, c))
    return pl.pallas_call(
        functools.partial(_rw_scan_kernel, tt, reverse),
        grid=(chains // LANES, nb),
        in_specs=[block] * 6 + [pl.BlockSpec((RW_HEAD, LANES), lambda c, s: (0, c))],
        out_specs=block,
        out_shape=jax.ShapeDtypeStruct((s_len, RW_HEAD, chains), F32),
        scratch_shapes=[pltpu.VMEM((RW_HEAD, RW_HEAD, LANES), F32), pltpu.VMEM((tt, RW_HEAD, LANES), F32),
                        pltpu.VMEM((tt, RW_HEAD, LANES), F32)],
        compiler_params=_cparams("arbitrary", "arbitrary"),
        name="rw_scan",
    )(r, k, v, kk, w, a, ka)


def _to_chains(u):
    s = u.shape[0]
    return u.reshape(s, -1, RW_HEAD).swapaxes(1, 2)


def _from_chains(y):
    return y.swapaxes(1, 2).reshape(y.shape[0], -1)


def _conv_kernel(g, tile, cur_ref, prev_ref, next_ref, w_ref, b_ref, o_ref, scr):
    first, last = _seq_pos(pl.program_id(0), tile, g)
    _fill_halo(scr, cur_ref, prev_ref, next_ref, first, last, tile)
    acc = b_ref[...] + w_ref[pl.ds(0, 1), :] * scr[pl.ds(HALO - 2, tile), :]
    for j in range(1, SSM_CONV):
        acc = acc + w_ref[pl.ds(j, 1), :] * scr[pl.ds(HALO - 2 + j, tile), :]
    o_ref[...] = _silu(acc)


def ssm_conv(g, xbc, conv_w, conv_b, tile):
    prev, nxt = _halo_specs(g, tile, XBC_DIM)
    row = pl.BlockSpec((tile, XBC_DIM), lambda r: (r, 0))
    return pl.pallas_call(
        functools.partial(_conv_kernel, g, tile),
        grid=(g.n // tile,),
        in_specs=[row, prev, nxt, _full((SSM_CONV, XBC_DIM)), _full((1, XBC_DIM))],
        out_specs=row,
        out_shape=jax.ShapeDtypeStruct((g.n, XBC_DIM), F32),
        scratch_shapes=[pltpu.VMEM((tile + 2 * HALO, XBC_DIM), F32)],
        compiler_params=_cparams("arbitrary"),
        name="ssm_conv",
    )(xbc, xbc, xbc, conv_w, conv_b.reshape(1, -1))


def _ssd_dir(rev, xbc, dtr, bias, aneg, expand, tri, st_ref):
    cl = xbc.shape[0]
    xs = xbc[:, 0:SSM_DIM]
    dt = _softplus(dtr + bias)
    da = dt * aneg
    cs = _xdot_r(tri, da)
    dt_x = _xdot_l(dt, expand)
    cs_x = _xdot_l(cs, expand)
    cs_end = cs_x[0:1, :] if rev else cs_x[cl - 1:cl, :]
    xdt = xs * dt_x
    cs_t = cs.T
    row = lax.broadcasted_iota(jnp.int32, (cl, cl), 0)
    col = lax.broadcasted_iota(jnp.int32, (cl, cl), 1)
    keep = (col >= row) if rev else (col <= row)
    lane = lax.broadcasted_iota(jnp.int32, (cl, LANES), 1)
    per_group = SSM_DIM // SSM_GROUPS
    y_parts = []
    for gi in range(SSM_GROUPS):
        bm = xbc[:, SSM_DIM + gi * SSM_STATE:SSM_DIM + (gi + 1) * SSM_STATE]
        cm = xbc[:, SSM_DIM + SSM_GROUPS * SSM_STATE + gi * SSM_STATE:
                 SSM_DIM + SSM_GROUPS * SSM_STATE + (gi + 1) * SSM_STATE]
        cb = lax.dot_general(cm.astype(BF16), bm.astype(BF16), (((1,), (1,)), ((), ())),
                             preferred_element_type=F32)
        gs = slice(gi * per_group, (gi + 1) * per_group)
        s_in = st_ref[:, gs]
        y_off = _bdot(cm, s_in) * jnp.exp(cs_x[:, gs])
        for p in range(per_group // LANES):
            h0 = (gi * per_group + p * LANES) // SSM_HEAD
            x_pair = xdt[:, h0 * SSM_HEAD:h0 * SSM_HEAD + LANES].astype(BF16)
            ys = []
            for hh in (h0, h0 + 1):
                seg = cs[:, hh:hh + 1] - cs_t[hh:hh + 1, :]
                m = cb * jnp.exp(jnp.where(keep, seg, -jnp.inf))
                ys.append(jnp.dot(m.astype(BF16), x_pair, preferred_element_type=F32))
            y_parts.append(jnp.where(lane < SSM_HEAD, ys[0], ys[1]) + y_off[:, p * LANES:(p + 1) * LANES])
        xd = (xdt[:, gs] * jnp.exp(cs_end[:, gs] - cs_x[:, gs])).astype(BF16)
        st_new = lax.dot_general(bm.astype(BF16), xd, (((0,), (0,)), ((), ())), preferred_element_type=F32)
        st_ref[:, gs] = s_in * jnp.exp(cs_end[:, gs]) + st_new
    return jnp.concatenate(y_parts, axis=1)


def _ssd_kernel(xf_ref, dtf_ref, xb_ref, dtb_ref, bias_ref, aneg_ref, exp_ref, tril_ref, triu_ref,
                yf_ref, yb_ref, st_scr):
    @pl.when(pl.program_id(1) == 0)
    def _():
        st_scr[...] = jnp.zeros_like(st_scr)

    yf_ref[...] = _ssd_dir(False, xf_ref[...], dtf_ref[...], bias_ref[0], aneg_ref[0], exp_ref[...],
                           tril_ref[...], st_scr.at[0])
    yb_ref[...] = _ssd_dir(True, xb_ref[...], dtb_ref[...], bias_ref[1], aneg_ref[1], exp_ref[...],
                           triu_ref[...], st_scr.at[1])


def ssd_scan(g, xbc_act, dt_f, dt_b, dt_bias, a_log, cl):
    ncc, nlc = g.c // cl, g.t // cl
    ns = ncc + nlc

    def rows(b, c):
        return jnp.where(c < ncc, g.nl // cl + b * ncc + c, b * nlc + (c - ncc))

    def fwd(b, s):
        return rows(b, s)

    def bwd(b, s):
        return rows(b, jnp.where(s < ncc, ncc - 1 - s, ns - 1 - (s - ncc)))

    pad = lambda u: jnp.pad(u, ((0, 0), (0, LANES - SSM_HEADS))).reshape(2, 1, LANES)
    expand = jnp.asarray(np.arange(LANES)[:, None] == (np.arange(SSM_DIM) // SSM_HEAD)[None, :], BF16)
    idx = np.arange(cl)
    tril = jnp.asarray(idx[:, None] >= idx[None, :], BF16)
    triu = jnp.asarray(idx[:, None] <= idx[None, :], BF16)
    spec = lambda w, f: pl.BlockSpec((cl, w), lambda b, s: (f(b, s), 0))
    return pl.pallas_call(
        _ssd_kernel,
        grid=(g.b, ns),
        in_specs=[spec(XBC_DIM, fwd), spec(LANES, fwd), spec(XBC_DIM, bwd), spec(LANES, bwd),
                  _full((2, 1, LANES)), _full((2, 1, LANES)), _full((LANES, SSM_DIM)), _full((cl, cl)),
                  _full((cl, cl))],
        out_specs=[spec(SSM_DIM, fwd), spec(SSM_DIM, bwd)],
        out_shape=[jax.ShapeDtypeStruct((g.n, SSM_DIM), F32)] * 2,
        scratch_shapes=[pltpu.VMEM((2, SSM_STATE, SSM_DIM), F32)],
        compiler_params=_cparams("arbitrary", "arbitrary"),
        name="ssd_scan",
    )(xbc_act, dt_f, xbc_act, dt_b, pad(dt_bias), pad(-jnp.exp(a_log)), expand, tril, triu)


def _even_out_kernel(x_ref, gate_ref, yf_ref, yb_ref, r_ref, k_ref, v_ref, af_ref, ab_ref, g_ref,
                     ysf_ref, ysb_ref, xs_ref, z_ref, ka_ref, rk_ref, gnw_ref, gnb_ref, ones_ref,
                     dsk_ref, nw_ref, wo_rw_ref, wo_ss_ref, o_ref):
    ones = ones_ref[...]
    y = yf_ref[...] + yb_ref[...]
    mean = _xdot_l(y, ones) * (1.0 / RW_HEAD)
    dev = y - mean
    var = _xdot_l(dev * dev, ones) * (1.0 / RW_HEAD)
    yn = dev * lax.rsqrt(var + RW_GN_EPS) * gnw_ref[...] + gnb_ref[...]
    r, k, ka = r_ref[...], k_ref[...], ka_ref[...]
    ksum = k * (1.0 + (af_ref[...] - 1.0) * ka) + k * (1.0 + (ab_ref[...] - 1.0) * ka)
    bonus = _xdot_l(r * ksum * rk_ref[...], ones) * v_ref[...]
    out_rw = (yn + bonus) * g_ref[...]

    ys = ysf_ref[...] + ysb_ref[...] + xs_ref[...] * dsk_ref[...]
    ys = ys * _silu(z_ref[...])
    half = SSM_DIM // SSM_GROUPS
    parts = []
    for gi in range(SSM_GROUPS):
        yg = ys[:, gi * half:(gi + 1) * half]
        parts.append(yg * lax.rsqrt(jnp.mean(yg * yg, axis=-1, keepdims=True) + NORM_EPS))
    out_ss = jnp.concatenate(parts, axis=1) * nw_ref[...]
    o = _bdot(out_rw, wo_rw_ref[...]) + _bdot(out_ss, wo_ss_ref[...])
    o_ref[...] = x_ref[...] + gate_ref[0] * o


def even_out(g, xa, mods, y_f, y_b, r, k, v, a_f, a_b, gg, ys_f, ys_b, xbc_act, z, ka, rk, gn_w, gn_b,
             d_skip, norm_w, w_out, tile=256):
    tile = _row_tile(g, tile, within_seq=True)
    row = lambda w: pl.BlockSpec((tile, w), lambda i: (i, 0))
    seq = _time_major_spec(g, tile)
    vec = lambda w: _full((1, w))
    return pl.pallas_call(
        _even_out_kernel,
        grid=(g.n // tile,),
        in_specs=[row(D_MODEL), _mod_spec(g, tile, 2)] + [seq] * 8 + [row(1024), row(1024), row(1024), row(1024),
                  vec(512), vec(512), vec(512), vec(512), _full((512, 512)), vec(1024), vec(1024),
                  _full((RW_DIM, D_MODEL)), _full((SSM_DIM, D_MODEL))],
        out_specs=row(D_MODEL),
        out_shape=jax.ShapeDtypeStruct((g.n, D_MODEL), F32),
        compiler_params=_cparams("arbitrary"),
        name="even_out",
    )(xa, mods, y_f, y_b, r, k, v, a_f, a_b, gg, ys_f, ys_b, xbc_act, z, ka.reshape(1, -1), rk.reshape(1, -1),
      gn_w.reshape(1, -1), gn_b.reshape(1, -1), _head_ones(), jnp.repeat(d_skip, SSM_HEAD).reshape(1, -1),
      norm_w.reshape(1, -1), w_out[:RW_DIM].astype(BF16), w_out[RW_DIM:].astype(BF16))


def even_layer(g, xa, mods, norm_g, w_in, w_out, mu, w0, w2, a0, a2, g2, k_k, k_a, r_k, gn_w, gn_b,
               conv_w, conv_b, dt_bias, a_log, d_skip, ssm_norm_w):
    o_z, o_x, o_dt = RW_COLS, RW_COLS + SSM_DIM, RW_COLS + SSM_DIM + XBC_DIM
    padw = lambda w: jnp.pad(w, ((0, 0), (0, LANES - w.shape[1])))
    weights = [w_in[:, :o_z], w_in[:, o_z:o_x], w_in[:, o_x:o_dt], padw(w_in[:, o_dt:o_dt + SSM_HEADS]),
               padw(w_in[:, o_dt + SSM_HEADS:])]
    rw, z, xbc, dt_f, dt_b = normmod_mm(g, xa, g.n, norm_g, mods, 0, 1, [w.astype(BF16) for w in weights])

    seq_tile = _row_tile(g, 256, within_seq=True)
    r, k, v, kk, w_f, w_b, a_f, a_b, gg = rw_prep(g, rw, mu, w0, w2, a0, a2, g2, k_k, seq_tile)
    ka_c = jnp.tile(k_a.reshape(RW_HEADS, RW_HEAD).T, (1, g.b))
    rc, kc, vc, kkc = (_to_chains(u) for u in (r, k, v, kk))
    tt = math.gcd(32, g.c)
    y_f = _from_chains(rw_scan(False, rc, kc, vc, kkc, _to_chains(w_f), _to_chains(a_f), ka_c, g.c, tt))
    y_b = _from_chains(rw_scan(True, rc, kc, vc, kkc, _to_chains(w_b), _to_chains(a_b), ka_c, g.c, tt))

    xbc_act = ssm_conv(g, xbc, conv_w, conv_b, seq_tile)
    ys_f, ys_b = ssd_scan(g, xbc_act, dt_f, dt_b, dt_bias, a_log, math.gcd(128, g.c))
    return even_out(g, xa, mods, y_f, y_b, r, k, v, a_f, a_b, gg, ys_f, ys_b, xbc_act, z, k_a, r_k, gn_w, gn_b,
                    d_skip, ssm_norm_w, w_out)


def _router_kernel(x_ref, g_ref, sh_ref, sc_ref, rw_ref, rb_ref, h_ref, gates_ref):
    h = _normmod(x_ref[...], g_ref[...], sh_ref[0], sc_ref[0])
    h_ref[...] = h.astype(BF16)
    nt = (((1,), (1,)), ((), ()))
    h3, w3 = _split3(h), _split3(rw_ref[...])
    logits = sum(lax.dot_general(w3[i], h3[j], nt, preferred_element_type=F32)
                 for i in range(3) for j in range(3 - i))
    scores = _sigmoid(logits)
    sel = scores + rb_ref[...]
    tile = sel.shape[1]
    per = N_EXPERTS // N_EXPERT_GROUPS
    grp = sel.reshape(N_EXPERT_GROUPS, per, tile)
    pos = lax.broadcasted_iota(jnp.int32, grp.shape, 1).astype(F32)
    m1 = jnp.max(grp, axis=1, keepdims=True)
    first = jnp.min(jnp.where(grp == m1, pos, float(per)), axis=1, keepdims=True)
    m2 = jnp.max(jnp.where(pos == first, -jnp.inf, grp), axis=1, keepdims=True)
    gsc = jnp.broadcast_to(m1 + m2, grp.shape)
    gid = lax.broadcasted_iota(jnp.int32, grp.shape, 0)
    beaten = jnp.zeros(grp.shape, F32)
    for o in range(N_EXPERT_GROUPS):
        other = gsc[o:o + 1]
        beaten = beaten + jnp.where((other > gsc) | ((other == gsc) & (o < gid)), 1.0, 0.0)
    cand = jnp.where(beaten < TOPK_GROUPS, grp, -jnp.inf).reshape(N_EXPERTS, tile)
    eid = lax.broadcasted_iota(jnp.int32, cand.shape, 0).astype(F32)
    chosen = jnp.zeros(cand.shape, F32)
    for _ in range(TOP_K):
        best = jnp.max(cand, axis=0, keepdims=True)
        pick = eid == jnp.min(jnp.where(cand == best, eid, float(N_EXPERTS)), axis=0, keepdims=True)
        chosen = jnp.where(pick, 1.0, chosen)
        cand = jnp.where(pick, -jnp.inf, cand)
    wts = scores * chosen
    gates = wts / jnp.sum(wts, axis=0, keepdims=True) * ROUTED_SCALE
    gates_ref[...] = jnp.concatenate([gates, jnp.zeros_like(gates)], axis=0).T


def moe_router(g, xa, nrows, gain, mods, router_w, router_bias, tile=512):
    d = xa.shape[1]
    tile = _row_tile(g, tile)
    return pl.pallas_call(
        _router_kernel,
        grid=(nrows // tile,),
        in_specs=[pl.BlockSpec((tile, d), lambda r: (r, 0)), _full((1, d)), _mod_spec(g, tile, 3),
                  _mod_spec(g, tile, 4), _full((N_EXPERTS, d)), _full((N_EXPERTS, 1))],
        out_specs=[pl.BlockSpec((tile, d), lambda r: (r, 0)), pl.BlockSpec((tile, LANES), lambda r: (r, 0))],
        out_shape=[jax.ShapeDtypeStruct((nrows, d), BF16), jax.ShapeDtypeStruct((nrows, LANES), F32)],
        compiler_params=_cparams("arbitrary"),
        name="moe_router",
    )(xa, gain.reshape(1, d), mods, mods, router_w.T, router_bias.reshape(N_EXPERTS, 1))


EXPERTS_PER_STEP = 4


def _experts_kernel(final, h_ref, gates_ref, w1_ref, w3_ref, w2_ref, sw1_ref, sw3_ref, sw2_ref, x_ref, gate_ref,
                    fin_ref, o_ref, acc):
    e = pl.program_id(1)
    h = h_ref[...]
    ff = EXPERT_FF

    def ffn(w1, w3):
        return _silu(jnp.dot(h, w1, preferred_element_type=F32)) * jnp.dot(h, w3, preferred_element_type=F32)

    @pl.when(e == 0)
    def _():
        acc[...] = jnp.dot(ffn(sw1_ref[...], sw3_ref[...]).astype(BF16), sw2_ref[...],
                           preferred_element_type=F32)

    first = e * EXPERTS_PER_STEP
    mine = pltpu.roll(gates_ref[...], jnp.where(first == 0, 0, LANES - first), axis=1)
    hid = jnp.concatenate([ffn(w1_ref[j], w3_ref[j]) * mine[:, j:j + 1] for j in range(EXPERTS_PER_STEP)], axis=1)
    acc[...] += jnp.dot(hid.astype(BF16), w2_ref[0], preferred_element_type=F32)

    @pl.when(e == pl.num_programs(1) - 1)
    def _():
        out = x_ref[...] + gate_ref[0] * acc[...]
        if final:
            ms = jnp.mean(out * out, axis=-1, keepdims=True)
            out = out * lax.rsqrt(ms + NORM_EPS) * fin_ref[...]
        o_ref[...] = out


def moe_experts(g, xa, nrows, mods, h, gates, w1, w3, w2, sw1, sw3, sw2, final_gain, final, tile=1024):
    d = xa.shape[1]
    tile = _row_tile(g, tile)
    eps = EXPERTS_PER_STEP
    w2c = w2.astype(BF16).reshape(N_EXPERTS // eps, eps * EXPERT_FF, d)
    up = pl.BlockSpec((eps, d, EXPERT_FF), lambda i, e: (e, 0, 0))
    return pl.pallas_call(
        functools.partial(_experts_kernel, final),
        grid=(nrows // tile, N_EXPERTS // eps),
        in_specs=[pl.BlockSpec((tile, d), lambda i, e: (i, 0)),
                  pl.BlockSpec((tile, LANES), lambda i, e: (i, 0)),
                  up, up, pl.BlockSpec((1, eps * EXPERT_FF, d), lambda i, e: (e, 0, 0)),
                  _full(sw1.shape), _full(sw3.shape), _full(sw2.shape),
                  pl.BlockSpec((tile, d), lambda i, e: (i, 0)), _mod_spec(g, tile, 5), _full((1, d))],
        out_specs=pl.BlockSpec((tile, d), lambda i, e: (i, 0)),
        out_shape=jax.ShapeDtypeStruct((nrows, d), F32),
        scratch_shapes=[pltpu.VMEM((tile, d), F32)],
        compiler_params=_cparams("arbitrary", "arbitrary"),
        name="moe_experts",
    )(h, gates, w1.astype(BF16), w3.astype(BF16), w2c, sw1.astype(BF16), sw3.astype(BF16), sw2.astype(BF16),
      xa, mods, final_gain.reshape(1, d))


def moe_layer(g, xa, nrows, mods, gain, router_w, router_bias, w1, w3, w2, sw1, sw3, sw2, final_gain, final):
    h, gates = moe_router(g, xa, nrows, gain, mods, router_w, router_bias)
    return moe_experts(g, xa, nrows, mods, h, gates, w1, w3, w2, sw1, sw3, sw2, final_gain, final)


HEAD_PAD = 128


def _rope_tables(g, scale):
    rows = g.t // GRID_W
    r_idx = jnp.repeat(jnp.arange(rows, dtype=F32), GRID_W)
    c_idx = jnp.tile(jnp.arange(GRID_W, dtype=F32), rows)
    axis_dim = MLA_ROPE // 2
    inv_freq = ROPE_THETA ** (-jnp.arange(0, axis_dim, 2, dtype=F32) / axis_dim)
    ang = jnp.concatenate([r_idx[:, None] * inv_freq, c_idx[:, None] * inv_freq], axis=-1)
    cos, sin = jnp.cos(ang), jnp.sin(ang)
    cos2 = jnp.concatenate([cos, cos], axis=1)
    sin2 = jnp.concatenate([-sin, sin], axis=1)
    t = g.t
    q_cos = jnp.concatenate([jnp.ones((t, MLA_NOPE), F32), cos2, jnp.zeros((t, 32), F32)], axis=1) * scale
    q_sin = jnp.concatenate([jnp.zeros((t, MLA_NOPE), F32), sin2, jnp.zeros((t, 32), F32)], axis=1) * scale
    zpad = jnp.zeros((t, LANES - MLA_ROPE), F32)
    k_cos = jnp.concatenate([cos2, zpad], axis=1)
    k_sin = jnp.concatenate([sin2, zpad], axis=1)
    c_cos = jnp.concatenate([jnp.ones((g.c, MLA_ROPE), F32), jnp.zeros((g.c, LANES - MLA_ROPE), F32)], axis=1)
    k_cos = jnp.concatenate([k_cos, c_cos], axis=0)
    k_sin = jnp.concatenate([k_sin, jnp.zeros((g.c, LANES), F32)], axis=0)
    return q_cos, q_sin, k_cos, k_sin


def _q_kernel(cq_ref, qn_ref, w1_ref, w2_ref, cos_ref, sin_ref, q_ref):
    cq = cq_ref[...]
    qn = (cq * lax.rsqrt(jnp.mean(cq * cq, axis=-1, keepdims=True) + NORM_EPS) * qn_ref[...]).astype(BF16)
    q1 = jnp.dot(qn, w1_ref[...], preferred_element_type=F32)
    q2 = jnp.dot(qn, w2_ref[...], preferred_element_type=F32)
    cos, sin = cos_ref[...], sin_ref[...]
    for h in range(MLA_HEADS):
        sl = slice(h * HEAD_PAD, (h + 1) * HEAD_PAD)
        q_ref[:, sl] = (q1[:, sl] * cos + q2[:, sl] * sin).astype(BF16)


def mla_queries(g, cq, q_norm, w1, w2, q_cos, q_sin, tile):
    per = g.t // tile
    width = MLA_HEADS * HEAD_PAD
    return pl.pallas_call(
        _q_kernel,
        grid=(g.nl // tile,),
        in_specs=[pl.BlockSpec((tile, MLA_Q_LORA), lambda r: (r, 0)), _full((1, MLA_Q_LORA)),
                  _full(w1.shape), _full(w2.shape),
                  pl.BlockSpec((tile, HEAD_PAD), lambda r: (r % per, 0)),
                  pl.BlockSpec((tile, HEAD_PAD), lambda r: (r % per, 0))],
        out_specs=pl.BlockSpec((tile, width), lambda r: (r, 0)),
        out_shape=jax.ShapeDtypeStruct((g.nl, width), BF16),
        compiler_params=_cparams("arbitrary"),
        name="mla_queries",
    )(cq, q_norm.reshape(1, -1), w1, w2, q_cos, q_sin)


def _kv_kernel(ckv_ref, kr1_ref, kr2_ref, kvn_ref, wk_ref, wv_ref, place_ref, cos_ref, sin_ref, k_ref, v_ref):
    ckv = ckv_ref[...]
    kvn = (ckv * lax.rsqrt(jnp.mean(ckv * ckv, axis=-1, keepdims=True) + NORM_EPS) * kvn_ref[...]).astype(BF16)
    kr = (kr1_ref[...] * cos_ref[...] + kr2_ref[...] * sin_ref[...]).astype(BF16)
    k = jnp.dot(kvn, wk_ref[...], preferred_element_type=F32) + jnp.dot(kr, place_ref[...], preferred_element_type=F32)
    k_ref[...] = k.astype(BF16)
    v_ref[...] = jnp.dot(kvn, wv_ref[...], preferred_element_type=F32).astype(BF16)


def mla_keys_values(g, ckv, kr1, kr2, kv_norm, wk, wv, place, k_cos, k_sin, tile):
    n_lat, per_l, per_c = g.nl // tile, g.t // tile, g.c // tile

    def tab(r):
        return (jnp.where(r < n_lat, r % per_l, per_l + (r - n_lat) % per_c), 0)

    row = lambda w: pl.BlockSpec((tile, w), lambda r: (r, 0))
    kw = MLA_HEADS * HEAD_PAD
    return pl.pallas_call(
        _kv_kernel,
        grid=(g.n // tile,),
        in_specs=[row(MLA_KV_LORA), row(LANES), row(LANES), _full((1, MLA_KV_LORA)), _full(wk.shape),
                  _full(wv.shape), _full(place.shape), pl.BlockSpec((tile, LANES), tab),
                  pl.BlockSpec((tile, LANES), tab)],
        out_specs=[row(kw), row(MLA_HEADS * MLA_V)],
        out_shape=[jax.ShapeDtypeStruct((g.n, kw), BF16), jax.ShapeDtypeStruct((g.n, MLA_HEADS * MLA_V), BF16)],
        compiler_params=_cparams("arbitrary"),
        name="mla_keys_values",
    )(ckv, kr1, kr2, kv_norm.reshape(1, -1), wk, wv, place, k_cos, k_sin)


def _attn_kernel(q_ref, kl_ref, kc_ref, vl_ref, vc_ref, o_ref):
    nt = (((1,), (1,)), ((), ()))
    vl, vc = vl_ref[...], vc_ref[...]
    heads = [slice(hh * HEAD_PAD, (hh + 1) * HEAD_PAD) for hh in range(2)]
    scores = [(lax.dot_general(q_ref[:, sl], kl_ref[:, sl], nt, preferred_element_type=F32),
               lax.dot_general(q_ref[:, sl], kc_ref[:, sl], nt, preferred_element_type=F32)) for sl in heads]
    outs = []
    for s_l, s_c in scores:
        m = jnp.maximum(jnp.max(s_l, axis=-1, keepdims=True), jnp.max(s_c, axis=-1, keepdims=True))
        p_l, p_c = jnp.exp2(s_l - m), jnp.exp2(s_c - m)
        den = jnp.sum(p_l, axis=-1, keepdims=True) + jnp.sum(p_c, axis=-1, keepdims=True)
        o = (jnp.dot(p_l.astype(BF16), vl, preferred_element_type=F32)
             + jnp.dot(p_c.astype(BF16), vc, preferred_element_type=F32))
        outs.append(o / den)
    lane = lax.broadcasted_iota(jnp.int32, outs[0].shape, 1)
    o_ref[...] = jnp.where(lane < MLA_V, outs[0], outs[1]).astype(o_ref.dtype)


def mla_attention(g, q, k, v, tq):
    nq = g.t // tq
    pairs = MLA_HEADS // 2
    pw = 2 * HEAD_PAD
    ctx0 = g.nl // g.c
    return pl.pallas_call(
        _attn_kernel,
        grid=(g.b, pairs, nq),
        in_specs=[pl.BlockSpec((tq, pw), lambda b, p, i: (b * nq + i, p)),
                  pl.BlockSpec((g.t, pw), lambda b, p, i: (b, p)),
                  pl.BlockSpec((g.c, pw), lambda b, p, i: (ctx0 + b, p)),
                  pl.BlockSpec((g.t, 2 * MLA_V), lambda b, p, i: (b, p)),
                  pl.BlockSpec((g.c, 2 * MLA_V), lambda b, p, i: (ctx0 + b, p))],
        out_specs=pl.BlockSpec((tq, 2 * MLA_V), lambda b, p, i: (b * nq + i, p)),
        out_shape=jax.ShapeDtypeStruct((g.nl, MLA_HEADS * MLA_V), BF16),
        compiler_params=_cparams("arbitrary", "arbitrary", "arbitrary"),
        name="mla_attention",
    )(q, k, k, v, v)


def _proj_resid_kernel(x_ref, gate_ref, a_ref, w_ref, o_ref):
    o_ref[...] = x_ref[...] + gate_ref[0] * jnp.dot(a_ref[...], w_ref[...], preferred_element_type=F32)


def proj_resid(g, xa, nrows, mods, act, w, tile=512):
    d = xa.shape[1]
    tile = _row_tile(g, tile)
    return pl.pallas_call(
        _proj_resid_kernel,
        grid=(nrows // tile,),
        in_specs=[pl.BlockSpec((tile, d), lambda r: (r, 0)), _mod_spec(g, tile, 2),
                  pl.BlockSpec((tile, act.shape[1]), lambda r: (r, 0)), _full(w.shape)],
        out_specs=pl.BlockSpec((tile, d), lambda r: (r, 0)),
        out_shape=jax.ShapeDtypeStruct((nrows, d), F32),
        compiler_params=_cparams("arbitrary"),
        name="proj_resid",
    )(xa, mods, act, w)


def mla_layer(g, xa, mods, norm_g, w_in, q_norm, q_up, kv_norm, kv_up, w_out):
    scale = (MLA_NOPE + MLA_ROPE) ** -0.5 * math.log2(math.e)
    half = MLA_ROPE // 2
    o_kv, o_kr = MLA_Q_LORA, MLA_Q_LORA + MLA_KV_LORA
    w_kr = w_in[:, o_kr:]
    padw = lambda w: jnp.pad(w, ((0, 0), (0, LANES - w.shape[1])))
    weights = [w_in[:, :o_kv], w_in[:, o_kv:o_kr], padw(w_kr),
               padw(jnp.concatenate([w_kr[:, half:], w_kr[:, :half]], axis=1))]
    cq, ckv, kr1, kr2 = normmod_mm(g, xa, g.n, norm_g, mods, 0, 1, [w.astype(BF16) for w in weights])

    qu = q_up.reshape(MLA_Q_LORA, MLA_HEADS, MLA_NOPE + MLA_ROPE)
    zq = jnp.zeros((MLA_Q_LORA, MLA_HEADS, HEAD_PAD - MLA_NOPE - MLA_ROPE), F32)
    w1 = jnp.concatenate([qu, zq], axis=2).reshape(MLA_Q_LORA, -1).astype(BF16)
    swapped = jnp.concatenate([qu[:, :, MLA_NOPE + half:], qu[:, :, MLA_NOPE:MLA_NOPE + half]], axis=2)
    w2 = jnp.concatenate([jnp.zeros((MLA_Q_LORA, MLA_HEADS, MLA_NOPE), F32), swapped, zq], axis=2)
    w2 = w2.reshape(MLA_Q_LORA, -1).astype(BF16)
    kvu = kv_up.reshape(MLA_KV_LORA, MLA_HEADS, MLA_NOPE + MLA_V)
    wk = jnp.concatenate([kvu[:, :, :MLA_NOPE], jnp.zeros((MLA_KV_LORA, MLA_HEADS, HEAD_PAD - MLA_NOPE), F32)],
                         axis=2).reshape(MLA_KV_LORA, -1).astype(BF16)
    wv = kvu[:, :, MLA_NOPE:].reshape(MLA_KV_LORA, -1).astype(BF16)
    col = np.arange(MLA_HEADS * HEAD_PAD)
    place = jnp.asarray((col[None, :] % HEAD_PAD) == (np.arange(LANES)[:, None] + MLA_NOPE), BF16)
    place = place * jnp.asarray(np.arange(LANES)[:, None] < MLA_ROPE, BF16)

    q_cos, q_sin, k_cos, k_sin = _rope_tables(g, scale)
    tile = _row_tile(g, 256, within_seq=True)
    q = mla_queries(g, cq, q_norm, w1, w2, q_cos, q_sin, tile)
    k, v = mla_keys_values(g, ckv, kr1, kr2, kv_norm, wk, wv, place, k_cos, k_sin, tile)
    attn = mla_attention(g, q, k, v, math.gcd(512, g.t))
    return proj_resid(g, xa, g.nl, mods, attn, w_out.astype(BF16))


def kernel(x, c, ctx, c_ctx, ada_w, ada_b, norm_mix, norm_ffn, ev_w_in, ev_w_out, rw_mu, rw_w0, rw_w2, rw_a0, rw_a2, rw_g2, rw_kk, rw_ka, rw_rk, rw_gn_w, rw_gn_b, ssm_conv_w, ssm_conv_b, ssm_dt_bias, ssm_a_log, ssm_d, ssm_norm_w, mla_w_in, mla_q_norm, mla_q_up, mla_kv_norm, mla_kv_up, mla_w_out, router_w, router_bias, exp_w1, exp_w3, exp_w2, sh_w1, sh_w3, sh_w2, final_norm):
    bsz, t, d = x.shape
    g = Geom(bsz, t, ctx.shape[1])
    depth = ada_w.shape[0]
    assert depth == 2 and d == D_MODEL, "layer 0 is the recurrent mixer, layer 1 the final attention layer"

    rows = -(-(bsz + 1) // SUBLANES) * SUBLANES
    cond = jnp.concatenate([c, c_ctx[None, :], jnp.zeros((rows - bsz - 1, d), F32)], axis=0)
    mods = ada_mods(cond, ada_w, ada_b).reshape(depth, rows, 1, 6 * d)
    xa = jnp.concatenate([x.reshape(g.nl, d), ctx.reshape(g.nc, d)], axis=0)

    xa = even_layer(g, xa, mods[0], norm_mix[0], ev_w_in[0], ev_w_out[0], rw_mu[0], rw_w0[0], rw_w2[0], rw_a0[0],
                    rw_a2[0], rw_g2[0], rw_kk[0], rw_ka[0], rw_rk[0], rw_gn_w[0], rw_gn_b[0], ssm_conv_w[0],
                    ssm_conv_b[0], ssm_dt_bias[0], ssm_a_log[0], ssm_d[0], ssm_norm_w[0])
    xa = moe_layer(g, xa, g.n, mods[0], norm_ffn[0], router_w[0], router_bias[0], exp_w1[0], exp_w3[0], exp_w2[0],
                   sh_w1[0], sh_w3[0], sh_w2[0], final_norm, False)
    xl = mla_layer(g, xa, mods[1], norm_mix[1], mla_w_in[0], mla_q_norm[0], mla_q_up[0], mla_kv_norm[0],
                   mla_kv_up[0], mla_w_out[0])
    xl = moe_layer(g, xl, g.nl, mods[1], norm_ffn[1], router_w[1], router_bias[1], exp_w1[1], exp_w3[1], exp_w2[1],
                   sh_w1[1], sh_w3[1], sh_w2[1], final_norm, True)
    return xl.reshape(bsz, t, d)
```
